```python
import math
import jax, jax.numpy as jnp
from jax import lax
import numpy as np

D_MODEL = 1024
BATCH = 4
SEQ = 4096
DEPTH = 4

GRID_W = 64
CTX_LEN = 256
CHUNK = 2 * GRID_W
D_BR = D_MODEL
A_GROUPS = 4
CONV_W = 31
C_HEADS = 8
C_HEAD_DIM = D_BR // C_HEADS
CONV_C = 4
LRU_C = 8.0
N_BRANCH = 3
D_FF = 2816
N_MOD = 9
SPLITS = [D_BR * i for i in range(1, 7)]
OFF_XC = 4 * D_BR
D_IN = 6 * D_BR + N_BRANCH * D_MODEL
ALPHA = (2.0 * DEPTH) ** 0.25
BETA = (8.0 * DEPTH) ** -0.25
LN_EPS = 1e-5

kernel_name = "hybrid_gmlp_conformer_rglru_dit"


def _layer_norm(x, g, b):
    xf = x.astype(jnp.float32)
    mu = jnp.mean(xf, axis=-1, keepdims=True)
    var = jnp.mean(jnp.square(xf - mu), axis=-1, keepdims=True)
    y = (xf - mu) * lax.rsqrt(var + LN_EPS)
    return (y * g + b).astype(x.dtype)


def _modulate(x, m, k):
    return x * (1.0 + m[..., 3 * k + 1, :]) + m[..., 3 * k, :]


def _post_norm(x, sub, m, k, res_w, g, b):
    return _layer_norm(ALPHA * x + res_w * m[..., 3 * k + 2, :] * sub, g, b)


def _swiglu(h, w_up, w_down):
    gt, up = jnp.split(h @ w_up, 2, axis=-1)
    return (jax.nn.silu(gt) * up) @ w_down


def _depthwise_conv(x, w, b, pad):
    y = lax.conv_general_dilated(x, w[:, None, :], window_strides=(1,), padding=[pad],
                                 dimension_numbers=('NWC', 'WIO', 'NWC'),
                                 feature_group_count=x.shape[-1])
    return y + b


def _chunk_mlp(u, v, ln_g, ln_b, w_s, b_s):
    bsz, s, _ = v.shape
    vn = _layer_norm(v, ln_g, ln_b).reshape(bsz, s // CHUNK, CHUNK, A_GROUPS, D_BR // A_GROUPS)
    mixed = jnp.einsum('gpq,bnqgc->bnpgc', w_s, vn) + b_s.T[None, None, :, :, None]
    return u * mixed.reshape(bsz, s, D_BR)


def _conv_module(a, gte, w_dw, b_dw, ln_g, ln_b):
    h = a * jax.nn.sigmoid(gte)
    h = _depthwise_conv(h, w_dw, b_dw, (CONV_W // 2, CONV_W // 2))
    return jax.nn.silu(_layer_norm(h, ln_g, ln_b))


def _mixer_features(z, lp):
    u_a, v_a, a_b, g_b, x_c, g_c, gates = jnp.split(z, SPLITS, axis=-1)
    y_a = _chunk_mlp(u_a, v_a, lp['a_ln_g'], lp['a_ln_b'], lp['a_w_s'], lp['a_b_s'])
    y_b = _conv_module(a_b, g_b, lp['b_w_dw'], lp['b_b_dw'], lp['b_ln_g'], lp['b_ln_b'])
    xc = _depthwise_conv(x_c, lp['c_w_conv'], lp['c_b_conv'], (2, 1))
    return y_a, y_b, xc, g_c, gates


def _lru_coeffs(x, w_a, b_a, w_x, b_x, lam):
    bsz, s, _ = x.shape
    xh = x.reshape(bsz, s, C_HEADS, C_HEAD_DIM)
    r = jax.nn.sigmoid(jnp.einsum('bshi,hij->bshj', xh, w_a).reshape(bsz, s, D_BR) + b_a)
    i = jax.nn.sigmoid(jnp.einsum('bshi,hij->bshj', xh, w_x).reshape(bsz, s, D_BR) + b_x)
    log_a = -LRU_C * r.astype(jnp.float32) * jax.nn.softplus(-lam.astype(jnp.float32))
    a = jnp.exp(log_a)
    inp = jnp.sqrt(-jnp.expm1(2.0 * log_a)) * (i * x).astype(jnp.float32)
    return a, inp


def _linear_scan(a, b, h0, reverse):
    if h0 is not None:
        edge = -1 if reverse else 0
        b = b.at[:, edge].add(a[:, edge] * h0)

    def comb(left, right):
        a_l, b_l = left
        a_r, b_r = right
        return a_l * a_r, a_r * b_l + b_r

    _, h = lax.associative_scan(comb, (a, b), axis=1, reverse=reverse)
    return h


def _rglru_bidir(x_lat, x_ctx, lp, ctx_out):
    y_lat = jnp.zeros(x_lat.shape, jnp.float32)
    y_ctx = jnp.zeros(x_ctx.shape, jnp.float32) if ctx_out else None
    for d, rev in enumerate((False, True)):
        prm = (lp['c_w_a'][d], lp['c_b_a'][d], lp['c_w_x'][d], lp['c_b_x'][d], lp['c_lam'][d])
        a_c, b_c = _lru_coeffs(x_ctx, *prm)
        h_c = _linear_scan(a_c, b_c, None, rev)
        h_end = h_c[:, 0] if rev else h_c[:, -1]
        a_l, b_l = _lru_coeffs(x_lat, *prm)
        y_lat = y_lat + _linear_scan(a_l, b_l, h_end, rev)
        if ctx_out:
            y_ctx = y_ctx + h_c
    return y_lat, y_ctx


def _merge(y_a, y_b, y_c, gates, w_br, w_o):
    bsz, s, _ = gates.shape
    g = jax.nn.sigmoid(gates).reshape(bsz, s, N_BRANCH, D_MODEL)
    ys = jnp.stack([y_a, y_b, y_c], axis=2)
    proj = jnp.einsum('bsnc,ncd->bsnd', ys, w_br)
    return jnp.einsum('bsnd,bsnd->bsd', proj, g) @ w_o


def _mixer(h_lat, h_ctx, lp, ctx_out):
    z_lat = h_lat @ lp['w_in'] + lp['b_in']
    y_a, y_b, xc_lat, gc_lat, gates_lat = _mixer_features(z_lat, lp)
    if ctx_out:
        z_ctx = h_ctx @ lp['w_in'] + lp['b_in']
        ya_c, yb_c, xc_ctx, gc_ctx, gates_ctx = _mixer_features(z_ctx, lp)
    else:
        zx = h_ctx @ lp['w_in'][:, OFF_XC:OFF_XC + D_BR] + lp['b_in'][OFF_XC:OFF_XC + D_BR]
        xc_ctx = _depthwise_conv(zx, lp['c_w_conv'], lp['c_b_conv'], (2, 1))
    r_lat, r_ctx = _rglru_bidir(xc_lat, xc_ctx, lp, ctx_out)
    yc_lat = (r_lat * jax.nn.gelu(gc_lat)).astype(h_lat.dtype)
    y_lat = _merge(y_a, y_b, yc_lat, gates_lat, lp['w_br'], lp['w_o'])
    if not ctx_out:
        return y_lat, None
    yc_ctx = (r_ctx * jax.nn.gelu(gc_ctx)).astype(h_ctx.dtype)
    y_ctx = _merge(ya_c, yb_c, yc_ctx, gates_ctx, lp['w_br'], lp['w_o'])
    return y_lat, y_ctx


def _normal(k, shape, scale):
    return jax.random.normal(k, shape, jnp.float32) * scale


def setup_inputs(seed: int = 0) -> dict:
    key = jax.random.key(seed)
    ks = jax.random.split(key, 32)
    L, D = DEPTH, D_MODEL
    u = jax.random.uniform(ks[28], (L, 2, D_BR), jnp.float32, 0.9, 0.999)
    sig = u ** (1.0 / LRU_C)
    c_lam = jnp.log(sig) - jnp.log1p(-sig)
    return {
        'x': _normal(ks[0], (BATCH, SEQ, D), 1.0),
        'c': _normal(ks[1], (BATCH, D), 1.0),
        'ctx': _normal(ks[2], (BATCH, CTX_LEN, D), 1.0),
        'c_ctx': _normal(ks[3], (D,), 1.0),
        'w_ada': _normal(ks[4], (L, D, N_MOD * D), 0.5 * D ** -0.5),
        'b_ada': _normal(ks[5], (L, N_MOD * D), 0.02),
        'ln_post_g': 1.0 + _normal(ks[6], (L, 3, D), 0.02),
        'ln_post_b': _normal(ks[7], (L, 3, D), 0.02),
        'ffn_w_up': _normal(ks[8], (L, 2, D, 2 * D_FF), D ** -0.5),
        'ffn_w_down': _normal(ks[9], (L, 2, D_FF, D), BETA * D_FF ** -0.5),
        'w_in': _normal(ks[10], (L, D, D_IN), D ** -0.5),
        'b_in': _normal(ks[11], (L, D_IN), 0.02),
        'a_ln_g': 1.0 + _normal(ks[12], (L, D_BR), 0.02),
        'a_ln_b': _normal(ks[13], (L, D_BR), 0.02),
        'a_w_s': _normal(ks[14], (L, A_GROUPS, CHUNK, CHUNK), CHUNK ** -0.5),
        'a_b_s': 1.0 + _normal(ks[15], (L, A_GROUPS, CHUNK), 0.02),
        'b_w_dw': _normal(ks[16], (L, CONV_W, D_BR), CONV_W ** -0.5),
        'b_b_dw': _normal(ks[17], (L, D_BR), 0.02),
        'b_ln_g': 1.0 + _normal(ks[18], (L, D_BR), 0.02),
        'b_ln_b': _normal(ks[19], (L, D_BR), 0.02),
        'c_w_conv': _normal(ks[20], (L, CONV_C, D_BR), CONV_C ** -0.5),
        'c_b_conv': _normal(ks[21], (L, D_BR), 0.02),
        'c_w_a': _normal(ks[22], (L, 2, C_HEADS, C_HEAD_DIM, C_HEAD_DIM), C_HEAD_DIM ** -0.5),
        'c_b_a': _normal(ks[23], (L, 2, D_BR), 0.02),
        'c_w_x': _normal(ks[24], (L, 2, C_HEADS, C_HEAD_DIM, C_HEAD_DIM), C_HEAD_DIM ** -0.5),
        'c_b_x': _normal(ks[25], (L, 2, D_BR), 0.02),
        'c_lam': c_lam,
        'w_br': _normal(ks[26], (L, N_BRANCH, D_BR, D), D_BR ** -0.5),
        'w_o': _normal(ks[27], (L, D, D), BETA * D ** -0.5),
    }


def reference(x, c, ctx, c_ctx, w_ada, b_ada, ln_post_g, ln_post_b, ffn_w_up, ffn_w_down,
              w_in, b_in, a_ln_g, a_ln_b, a_w_s, a_b_s, b_w_dw, b_b_dw, b_ln_g, b_ln_b,
              c_w_conv, c_b_conv, c_w_a, c_b_a, c_w_x, c_b_x, c_lam, w_br, w_o):
    bsz = x.shape[0]
    s_c = jax.nn.silu(c)
    s_cc = jax.nn.silu(c_ctx)
    for l in range(DEPTH):
        ctx_out = l < DEPTH - 1
        m_lat = (s_c @ w_ada[l] + b_ada[l]).reshape(bsz, 1, N_MOD, D_MODEL)
        m_ctx = (s_cc @ w_ada[l] + b_ada[l]).reshape(N_MOD, D_MODEL)
        lp = {
            'w_in': w_in[l], 'b_in': b_in[l],
            'a_ln_g': a_ln_g[l], 'a_ln_b': a_ln_b[l], 'a_w_s': a_w_s[l], 'a_b_s': a_b_s[l],
            'b_w_dw': b_w_dw[l], 'b_b_dw': b_b_dw[l], 'b_ln_g': b_ln_g[l], 'b_ln_b': b_ln_b[l],
            'c_w_conv': c_w_conv[l], 'c_b_conv': c_b_conv[l],
            'c_w_a': c_w_a[l], 'c_b_a': c_b_a[l], 'c_w_x': c_w_x[l], 'c_b_x': c_b_x[l],
            'c_lam': c_lam[l], 'w_br': w_br[l], 'w_o': w_o[l],
        }
        x = _post_norm(x, _swiglu(_modulate(x, m_lat, 0), ffn_w_up[l, 0], ffn_w_down[l, 0]),
                       m_lat, 0, 0.5, ln_post_g[l, 0], ln_post_b[l, 0])
        ctx = _post_norm(ctx, _swiglu(_modulate(ctx, m_ctx, 0), ffn_w_up[l, 0], ffn_w_down[l, 0]),
                         m_ctx, 0, 0.5, ln_post_g[l, 0], ln_post_b[l, 0])
        y_lat, y_ctx = _mixer(_modulate(x, m_lat, 1), _modulate(ctx, m_ctx, 1), lp, ctx_out)
        x = _post_norm(x, y_lat, m_lat, 1, 1.0, ln_post_g[l, 1], ln_post_b[l, 1])
        x = _post_norm(x, _swiglu(_modulate(x, m_lat, 2), ffn_w_up[l, 1], ffn_w_down[l, 1]),
                       m_lat, 2, 0.5, ln_post_g[l, 2], ln_post_b[l, 2])
        if ctx_out:
            ctx = _post_norm(ctx, y_ctx, m_ctx, 1, 1.0, ln_post_g[l, 1], ln_post_b[l, 1])
            ctx = _post_norm(ctx, _swiglu(_modulate(ctx, m_ctx, 2), ffn_w_up[l, 1], ffn_w_down[l, 1]),
                             m_ctx, 2, 0.5, ln_post_g[l, 2], ln_post_b[l, 2])
    return x
```

```python
import functools

import jax
import jax.numpy as jnp
from jax import lax
from jax.experimental import pallas as pl
from jax.experimental.pallas import tpu as pltpu

D = 1024
DEPTH = 4
CHUNK = 128
A_GROUPS = 4
A_GROUP_W = D // A_GROUPS
CONV_W = 31
CONV_HALF = CONV_W // 2
C_HEADS = 8
C_HEAD_DIM = D // C_HEADS
CONV_C = 4
LRU_C = 8.0
D_FF = 2816
N_MOD = 9
ALPHA = (2.0 * DEPTH) ** 0.25
LN_EPS = 1e-5

MOD_ROWS = 8
MOD_TN = 1152
FFN_TM = 512
FFN_CK = 256
HALO_B = 16
HALO_C = 8
VMEM_LIMIT = 60 * 1024 * 1024

F32 = jnp.float32
BF16 = jnp.bfloat16


def _dot(a, b):
    return jnp.dot(a, b, preferred_element_type=F32)


def _ln(v, g, b):
    mu = jnp.mean(v, axis=-1, keepdims=True)
    c = v - mu
    var = jnp.mean(c * c, axis=-1, keepdims=True)
    return c * lax.rsqrt(var + LN_EPS) * g + b


def _const_spec(shape):
    nd = len(shape)
    return pl.BlockSpec(shape, lambda *_: (0,) * nd, pipeline_mode=pl.Buffered(1))


def _params(n_grid):
    return pltpu.CompilerParams(
        dimension_semantics=("arbitrary",) * n_grid, vmem_limit_bytes=VMEM_LIMIT)


def _mod_kernel(cv_ref, w_ref, b_ref, o_ref):
    s = jax.nn.silu(cv_ref[...])
    o_ref[...] = jnp.dot(s, w_ref[...], precision=lax.Precision.HIGHEST,
                         preferred_element_type=F32) + b_ref[...]


def _modulation(cv, w_ada, b_ada):
    n_cols = N_MOD * D
    return pl.pallas_call(
        _mod_kernel,
        grid=(DEPTH, n_cols // MOD_TN),
        in_specs=[
            pl.BlockSpec((MOD_ROWS, D), lambda l, n: (0, 0)),
            pl.BlockSpec((None, D, MOD_TN), lambda l, n: (l, 0, n)),
            pl.BlockSpec((None, 1, MOD_TN), lambda l, n: (l, 0, n)),
        ],
        out_specs=pl.BlockSpec((None, MOD_ROWS, MOD_TN), lambda l, n: (l, 0, n)),
        out_shape=jax.ShapeDtypeStruct((DEPTH, MOD_ROWS, n_cols), F32),
        compiler_params=_params(2),
        name="adaln_mod",
    )(cv, w_ada, b_ada.reshape(DEPTH, 1, n_cols))


def _ffn_kernel(x_ref, mod_ref, wup_ref, wdn_ref, lng_ref, lnb_ref, o_ref, act_ref, *, k):
    x = x_ref[...]
    shift = mod_ref[3 * k:3 * k + 1, :]
    scale = mod_ref[3 * k + 1:3 * k + 2, :]
    gate = mod_ref[3 * k + 2:3 * k + 3, :]
    h = (x * (1.0 + scale) + shift).astype(BF16)
    for c in range(D_FF // FFN_CK):
        lo = c * FFN_CK
        gt = _dot(h, wup_ref[:, lo:lo + FFN_CK])
        up = _dot(h, wup_ref[:, D_FF + lo:D_FF + lo + FFN_CK])
        act_ref[:, lo:lo + FFN_CK] = (jax.nn.silu(gt) * up).astype(BF16)
    y = _dot(act_ref[...], wdn_ref[...])
    o_ref[...] = _ln(ALPHA * x + (0.5 * gate) * y, lng_ref[...], lnb_ref[...])


def _ffn(x2, mod_l, wup, wdn, lng, lnb, *, k, rows_per_mod, mod_base):
    n = x2.shape[0]
    tiles_per_mod = rows_per_mod // FFN_TM
    return pl.pallas_call(
        functools.partial(_ffn_kernel, k=k),
        grid=(n // FFN_TM,),
        in_specs=[
            pl.BlockSpec((FFN_TM, D), lambda j: (j, 0)),
            pl.BlockSpec((None, N_MOD, D), lambda j: (mod_base + j // tiles_per_mod, 0, 0)),
            _const_spec((D, 2 * D_FF)),
            _const_spec((D_FF, D)),
            _const_spec((1, D)),
            _const_spec((1, D)),
        ],
        out_specs=pl.BlockSpec((FFN_TM, D), lambda j: (j, 0)),
        out_shape=jax.ShapeDtypeStruct((n, D), F32),
        scratch_shapes=[pltpu.VMEM((FFN_TM, D_FF), BF16)],
        compiler_params=_params(1),
        name="ffn",
    )(x2, mod_l, wup, wdn, lng, lnb)


def _lru_coeffs(xcv_ref, wg_ref, bg_ref, lam_ref, a_ref, b_ref, d):
    nsp = -LRU_C * jax.nn.softplus(-lam_ref[d:d + 1, :])
    for hd in range(C_HEADS):
        sl = slice(hd * C_HEAD_DIM, (hd + 1) * C_HEAD_DIM)
        xh = xcv_ref[:, sl]
        g = _dot(xh.astype(BF16), wg_ref[d, hd])
        r = jax.nn.sigmoid(g[:, :C_HEAD_DIM] + bg_ref[2 * d:2 * d + 1, sl])
        i = jax.nn.sigmoid(g[:, C_HEAD_DIM:] + bg_ref[2 * d + 1:2 * d + 2, sl])
        a = jnp.exp(nsp[:, sl] * r)
        a_ref[:, sl] = a
        b_ref[:, sl] = jnp.sqrt(1.0 - a * a) * (i * xh)


def _scan_rows(a_ref, b_ref, out_ref, h_init, n_rows, reverse, accumulate):
    def body(s, h):
        t = n_rows - 1 - s if reverse else s
        h = a_ref[pl.ds(t, 1), :] * h + b_ref[pl.ds(t, 1), :]
        if accumulate:
            out_ref[pl.ds(t, 1), :] = out_ref[pl.ds(t, 1), :] + h
        else:
            out_ref[pl.ds(t, 1), :] = h
        return h
    return lax.fori_loop(0, n_rows, body, h_init, unroll=8)


def _scan_kernel(xl_ref, xc_ref, xr_ref, mod_ref, wxc_ref, bxc_ref, wcv_ref, bcv_ref,
                 wg_ref, bg_ref, lam_ref, h0f_ref, h0b_ref,
                 xcv_ref, hs_ref, endf_ref, endb_ref,
                 zx_ref, a_ref, b_ref, carry_ref, *, tile, n_tiles, dirs, reverse):
    j = pl.program_id(1)
    jj = n_tiles - 1 - j if reverse else j
    shift = mod_ref[3:4, :]
    scale = mod_ref[4:5, :]
    xe = jnp.concatenate([xl_ref[...], xc_ref[...], xr_ref[...]], axis=0)
    he = (xe * (1.0 + scale) + shift).astype(BF16)
    zx = _dot(he, wxc_ref[...]) + bxc_ref[...]
    row = lax.broadcasted_iota(jnp.int32, (tile + 2 * HALO_C, 1), 0)
    valid = jnp.logical_and(jnp.logical_or(row >= HALO_C, jj > 0),
                            jnp.logical_or(row < tile + HALO_C, jj < n_tiles - 1))
    zx_ref[...] = jnp.where(valid, zx, 0.0)
    acc = bcv_ref[...] + wcv_ref[0:1, :] * zx_ref[pl.ds(HALO_C - 2, tile), :]
    for k in range(1, CONV_C):
        acc = acc + wcv_ref[k:k + 1, :] * zx_ref[pl.ds(HALO_C - 2 + k, tile), :]
    xcv_ref[...] = acc

    for d, end_ref in ((0, endf_ref), (1, endb_ref)):
        if d not in dirs:
            end_ref[...] = jnp.zeros_like(end_ref)
    first = True
    for d in dirs:
        _lru_coeffs(xcv_ref, wg_ref, bg_ref, lam_ref, a_ref, b_ref, d)
        h0_ref = h0b_ref if d == 1 else h0f_ref
        end_ref = endb_ref if d == 1 else endf_ref

        @pl.when(j == 0)
        def _():
            carry_ref[d:d + 1, :] = h0_ref[...]

        h = _scan_rows(a_ref, b_ref, hs_ref, carry_ref[d:d + 1, :], tile,
                       reverse=(d == 1), accumulate=not first)
        carry_ref[d:d + 1, :] = h
        end_ref[...] = h
        first = False


def _scan_call(x3, mod_l, mod_base, wxc, bxc, wcv, bcv, wg, bg, lam, h0f, h0b,
               *, tile, dirs, reverse):
    bsz, s, _ = x3.shape
    n_tiles = s // tile
    hb = tile // HALO_C
    n_hblk = s // HALO_C

    def ctr(b, j):
        return (b, n_tiles - 1 - j if reverse else j, 0)

    def left(b, j):
        jj = n_tiles - 1 - j if reverse else j
        return (b, jnp.maximum(jj * hb - 1, 0), 0)

    def right(b, j):
        jj = n_tiles - 1 - j if reverse else j
        return (b, jnp.minimum((jj + 1) * hb, n_hblk - 1), 0)

    vec = pl.BlockSpec((None, 1, D), lambda b, j: (b, 0, 0))
    return pl.pallas_call(
        functools.partial(_scan_kernel, tile=tile, n_tiles=n_tiles, dirs=dirs, reverse=reverse),
        grid=(bsz, n_tiles),
        in_specs=[
            pl.BlockSpec((None, HALO_C, D), left),
            pl.BlockSpec((None, tile, D), ctr),
            pl.BlockSpec((None, HALO_C, D), right),
            pl.BlockSpec((None, N_MOD, D), lambda b, j: (mod_base(b), 0, 0)),
            _const_spec((D, D)),
            _const_spec((1, D)),
            _const_spec((CONV_C, D)),
            _const_spec((1, D)),
            _const_spec((2, C_HEADS, C_HEAD_DIM, 2 * C_HEAD_DIM)),
            _const_spec((4, D)),
            _const_spec((2, D)),
            vec, vec,
        ],
        out_specs=[
            pl.BlockSpec((None, tile, D), ctr),
            pl.BlockSpec((None, tile, D), ctr),
            vec, vec,
        ],
        out_shape=[
            jax.ShapeDtypeStruct((bsz, s, D), F32),
            jax.ShapeDtypeStruct((bsz, s, D), F32),
            jax.ShapeDtypeStruct((bsz, 1, D), F32),
            jax.ShapeDtypeStruct((bsz, 1, D), F32),
        ],
        scratch_shapes=[
            pltpu.VMEM((tile + 2 * HALO_C, D), F32),
            pltpu.VMEM((tile, D), F32),
            pltpu.VMEM((tile, D), F32),
            pltpu.VMEM((2, D), F32),
        ],
        compiler_params=_params(2),
        name="lru_scan",
    )(x3, x3, x3, mod_l, wxc, bxc, wcv, bcv, wg, bg, lam, h0f, h0b)


def _branch_kernel(xl_ref, xc_ref, xr_ref, mod_ref, win_ref, bin_ref,
                   alg_ref, alb_ref, ws_ref, bs_ref, wdw_ref, bdw_ref, blg_ref, blb_ref, wbr_ref,
                   part_ref, gcg_ref, sgc_ref,
                   hb_ref, ya_ref, *, tile, n_tiles):
    j = pl.program_id(1)
    shift = mod_ref[3:4, :]
    scale = mod_ref[4:5, :]
    xe = jnp.concatenate([xl_ref[...], xc_ref[...], xr_ref[...]], axis=0)
    he = (xe * (1.0 + scale) + shift).astype(BF16)
    hc = he[HALO_B:HALO_B + tile]

    a_b = _dot(he, win_ref[:, 2 * D:3 * D]) + bin_ref[:, 2 * D:3 * D]
    g_b = _dot(he, win_ref[:, 3 * D:4 * D]) + bin_ref[:, 3 * D:4 * D]
    row = lax.broadcasted_iota(jnp.int32, (tile + 2 * HALO_B, 1), 0)
    valid = jnp.logical_and(jnp.logical_or(row >= HALO_B, j > 0),
                            jnp.logical_or(row < tile + HALO_B, j < n_tiles - 1))
    hb_ref[...] = jnp.where(valid, a_b * jax.nn.sigmoid(g_b), 0.0)
    off = HALO_B - CONV_HALF
    acc = bdw_ref[...] + wdw_ref[0:1, :] * hb_ref[pl.ds(off, tile), :]
    for k in range(1, CONV_W):
        acc = acc + wdw_ref[k:k + 1, :] * hb_ref[pl.ds(off + k, tile), :]
    y_b = jax.nn.silu(_ln(acc, blg_ref[...], blb_ref[...])).astype(BF16)
    p_b = _dot(y_b, wbr_ref[1])

    u_a = _dot(hc, win_ref[:, 0:D]) + bin_ref[:, 0:D]
    v_a = _dot(hc, win_ref[:, D:2 * D]) + bin_ref[:, D:2 * D]
    vn = _ln(v_a, alg_ref[...], alb_ref[...]).astype(BF16)
    for n in range(tile // CHUNK):
        rows = slice(n * CHUNK, (n + 1) * CHUNK)
        mixed = jnp.concatenate(
            [_dot(ws_ref[g], vn[rows, g * A_GROUP_W:(g + 1) * A_GROUP_W]) for g in range(A_GROUPS)],
            axis=1)
        ya_ref[rows, :] = (u_a[rows, :] * (mixed + bs_ref[...])).astype(BF16)
    p_a = _dot(ya_ref[...], wbr_ref[0])

    sg_a = jax.nn.sigmoid(_dot(hc, win_ref[:, 6 * D:7 * D]) + bin_ref[:, 6 * D:7 * D])
    sg_b = jax.nn.sigmoid(_dot(hc, win_ref[:, 7 * D:8 * D]) + bin_ref[:, 7 * D:8 * D])
    part_ref[...] = sg_a * p_a + sg_b * p_b
    sgc_ref[...] = jax.nn.sigmoid(_dot(hc, win_ref[:, 8 * D:9 * D]) + bin_ref[:, 8 * D:9 * D])
    gcg_ref[...] = jax.nn.gelu(_dot(hc, win_ref[:, 5 * D:6 * D]) + bin_ref[:, 5 * D:6 * D])


def _branch_call(x3, mod_l, mod_base, win, bin_, alg, alb, ws, bs, wdw, bdw, blg, blb, wbr, *, tile):
    bsz, s, _ = x3.shape
    n_tiles = s // tile
    hb = tile // HALO_B
    n_hblk = s // HALO_B
    ctr = pl.BlockSpec((None, tile, D), lambda b, j: (b, j, 0))
    out = jax.ShapeDtypeStruct((bsz, s, D), F32)
    return pl.pallas_call(
        functools.partial(_branch_kernel, tile=tile, n_tiles=n_tiles),
        grid=(bsz, n_tiles),
        in_specs=[
            pl.BlockSpec((None, HALO_B, D), lambda b, j: (b, jnp.maximum(j * hb - 1, 0), 0)),
            ctr,
            pl.BlockSpec((None, HALO_B, D), lambda b, j: (b, jnp.minimum((j + 1) * hb, n_hblk - 1), 0)),
            pl.BlockSpec((None, N_MOD, D), lambda b, j: (mod_base(b), 0, 0)),
            _const_spec((D, N_MOD * D)),
            _const_spec((1, N_MOD * D)),
            _const_spec((1, D)),
            _const_spec((1, D)),
            _const_spec((A_GROUPS, CHUNK, CHUNK)),
            _const_spec((CHUNK, D)),
            _const_spec((CONV_W, D)),
            _const_spec((1, D)),
            _const_spec((1, D)),
            _const_spec((1, D)),
            _const_spec((2, D, D)),
        ],
        out_specs=[ctr, ctr, ctr],
        out_shape=[out, out, out],
        scratch_shapes=[
            pltpu.VMEM((tile + 2 * HALO_B, D), F32),
            pltpu.VMEM((tile, D), BF16),
        ],
        compiler_params=_params(2),
        name="mixer_branches",
    )(x3, x3, x3, mod_l, win, bin_, alg, alb, ws, bs, wdw, bdw, blg, blb, wbr)


def _merge_kernel(x_ref, xcv_ref, hs_ref, part_ref, gcg_ref, sgc_ref, mod_ref,
                  wg_ref, bg_ref, lam_ref, h0f_ref, wbc_ref, wo_ref, lng_ref, lnb_ref,
                  o_ref, a_ref, b_ref, hf_ref, carry_ref, *, tile, fwd_scan):
    j = pl.program_id(1)
    if fwd_scan:
        _lru_coeffs(xcv_ref, wg_ref, bg_ref, lam_ref, a_ref, b_ref, 0)

        @pl.when(j == 0)
        def _():
            carry_ref[...] = h0f_ref[...]

        carry_ref[...] = _scan_rows(a_ref, b_ref, hf_ref, carry_ref[...], tile,
                                    reverse=False, accumulate=False)
        r = hf_ref[...] + hs_ref[...]
    else:
        r = hs_ref[...]
    y_c = (r * gcg_ref[...]).astype(BF16)
    m = part_ref[...] + sgc_ref[...] * _dot(y_c, wbc_ref[...])
    y = _dot(m.astype(BF16), wo_ref[...])
    gate = mod_ref[5:6, :]
    o_ref[...] = _ln(ALPHA * x_ref[...] + gate * y, lng_ref[...], lnb_ref[...])


def _merge_call(x3, xcv, hs, part, gcg, sgc, mod_l, mod_base, wg, bg, lam, h0f, wbc, wo, lng, lnb,
                *, tile, fwd_scan):
    bsz, s, _ = x3.shape
    ctr = pl.BlockSpec((None, tile, D), lambda b, j: (b, j, 0))
    return pl.pallas_call(
        functools.partial(_merge_kernel, tile=tile, fwd_scan=fwd_scan),
        grid=(bsz, s // tile),
        in_specs=[
            ctr, ctr, ctr, ctr, ctr, ctr,
            pl.BlockSpec((None, N_MOD, D), lambda b, j: (mod_base(b), 0, 0)),
            _const_spec((2, C_HEADS, C_HEAD_DIM, 2 * C_HEAD_DIM)),
            _const_spec((4, D)),
            _const_spec((2, D)),
            pl.BlockSpec((None, 1, D), lambda b, j: (b, 0, 0)),
            _const_spec((D, D)),
            _const_spec((D, D)),
            _const_spec((1, D)),
            _const_spec((1, D)),
        ],
        out_specs=ctr,
        out_shape=jax.ShapeDtypeStruct((bsz, s, D), F32),
        scratch_shapes=[
            pltpu.VMEM((tile, D), F32),
            pltpu.VMEM((tile, D), F32),
            pltpu.VMEM((tile, D), F32),
            pltpu.VMEM((1, D), F32),
        ],
        compiler_params=_params(2),
        name="mixer_merge",
    )(x3, xcv, hs, part, gcg, sgc, mod_l, wg, bg, lam, h0f, wbc, wo, lng, lnb)


LAT_TILE = 512
CTX_TILE = 256


def kernel(x, c, ctx, c_ctx, w_ada, b_ada, ln_post_g, ln_post_b, ffn_w_up, ffn_w_down,
           w_in, b_in, a_ln_g, a_ln_b, a_w_s, a_b_s, b_w_dw, b_b_dw, b_ln_g, b_ln_b,
           c_w_conv, c_b_conv, c_w_a, c_b_a, c_w_x, c_b_x, c_lam, w_br, w_o):
    bsz, seq, _ = x.shape
    ctx_len = ctx.shape[1]

    cv = jnp.concatenate(
        [c_ctx[None, :], c, jnp.zeros((MOD_ROWS - 1 - bsz, D), F32)], axis=0)
    mod = _modulation(cv, w_ada, b_ada).reshape(DEPTH, MOD_ROWS, N_MOD, D)

    wup = ffn_w_up.astype(BF16)
    wdn = ffn_w_down.astype(BF16)
    win = w_in.astype(BF16)
    wbr = w_br.astype(BF16)
    wo = w_o.astype(BF16)
    ws = a_w_s.astype(BF16)
    wg = jnp.concatenate([c_w_a, c_w_x], axis=-1).astype(BF16)
    bg = jnp.stack([c_b_a[:, 0], c_b_x[:, 0], c_b_a[:, 1], c_b_x[:, 1]], axis=1)
    bs_full = jnp.repeat(jnp.swapaxes(a_b_s, 1, 2), A_GROUP_W, axis=2)
    row = lambda v: v.reshape(DEPTH, 1, -1)
    lat_mod = lambda b: 1 + b
    ctx_mod = lambda b: 0
    zeros_h = jnp.zeros((bsz, 1, D), F32)

    for l in range(DEPTH):
        ctx_out = l < DEPTH - 1
        mod_l = mod[l]
        lng = ln_post_g[l].reshape(3, 1, D)
        lnb = ln_post_b[l].reshape(3, 1, D)
        ffn = functools.partial(_ffn, mod_l=mod_l)

        x = ffn(x.reshape(bsz * seq, D), wup=wup[l, 0], wdn=wdn[l, 0], lng=lng[0], lnb=lnb[0],
                k=0, rows_per_mod=seq, mod_base=1).reshape(bsz, seq, D)
        ctx = ffn(ctx.reshape(bsz * ctx_len, D), wup=wup[l, 0], wdn=wdn[l, 0], lng=lng[0],
                  lnb=lnb[0], k=0, rows_per_mod=bsz * ctx_len, mod_base=0).reshape(bsz, ctx_len, D)

        scan_w = (win[l][:, 4 * D:5 * D], row(b_in[:, 4 * D:5 * D])[l], c_w_conv[l],
                  row(c_b_conv)[l], wg[l], bg[l], c_lam[l])
        _, hs_ctx, end_f, end_b = _scan_call(
            ctx, mod_l, ctx_mod, *scan_w, zeros_h, zeros_h,
            tile=CTX_TILE, dirs=(0, 1), reverse=False)
        xcv_lat, hb_lat, _, _ = _scan_call(
            x, mod_l, lat_mod, *scan_w, end_f, end_b,
            tile=LAT_TILE, dirs=(1,), reverse=True)

        br_w = (win[l], row(b_in)[l], row(a_ln_g)[l], row(a_ln_b)[l], ws[l], bs_full[l],
                b_w_dw[l], row(b_b_dw)[l], row(b_ln_g)[l], row(b_ln_b)[l], wbr[l, 0:2])
        mg_w = (wg[l], bg[l], c_lam[l])
        part, gcg, sgc = _branch_call(x, mod_l, lat_mod, *br_w, tile=LAT_TILE)
        x = _merge_call(x, xcv_lat, hb_lat, part, gcg, sgc, mod_l, lat_mod, *mg_w, end_f,
                        wbr[l, 2], wo[l], lng[1], lnb[1], tile=LAT_TILE, fwd_scan=True)
        x = ffn(x.reshape(bsz * seq, D), wup=wup[l, 1], wdn=wdn[l, 1], lng=lng[2], lnb=lnb[2],
                k=2, rows_per_mod=seq, mod_base=1).reshape(bsz, seq, D)

        if ctx_out:
            part, gcg, sgc = _branch_call(ctx, mod_l, ctx_mod, *br_w, tile=CTX_TILE)
            ctx = _merge_call(ctx, hs_ctx, hs_ctx, part, gcg, sgc, mod_l, ctx_mod, *mg_w, zeros_h,
                              wbr[l, 2], wo[l], lng[1], lnb[1], tile=CTX_TILE, fwd_scan=False)
            ctx = ffn(ctx.reshape(bsz * ctx_len, D), wup=wup[l, 1], wdn=wdn[l, 1], lng=lng[2],
                      lnb=lnb[2], k=2, rows_per_mod=bsz * ctx_len, mod_base=0
                      ).reshape(bsz, ctx_len, D)
    return x
```

```python
import functools

import jax
import jax.numpy as jnp
from jax import lax
from jax.experimental import pallas as pl
from jax.experimental.pallas import tpu as pltpu

D = 1024
DEPTH = 4
CHUNK = 128
A_GROUPS = 4
A_GROUP_W = D // A_GROUPS
CONV_W = 31
CONV_HALF = CONV_W // 2
C_HEADS = 8
C_HEAD_DIM = D // C_HEADS
CONV_C = 4
LRU_C = 8.0
D_FF = 2816
N_MOD = 9
ALPHA = (2.0 * DEPTH) ** 0.25
LN_EPS = 1e-5

SUBLANES = 8
MOD_ROWS = 8
MOD_TN = 1152
FFN_TM = 512
FFN_CK = 256
HALO_B = 16
HALO_C = 8
LAT_TILE = 512
CTX_TILE = 256
VMEM_LIMIT = 60 * 1024 * 1024

F32 = jnp.float32
BF16 = jnp.bfloat16


def _dot(a, b):
    return jnp.dot(a, b, preferred_element_type=F32)


def _ln(v, g, b):
    mu = jnp.mean(v, axis=-1, keepdims=True)
    c = v - mu
    var = jnp.mean(c * c, axis=-1, keepdims=True)
    return c * lax.rsqrt(var + LN_EPS) * g + b


def _sigmoid(v):
    return 0.5 * jnp.tanh(0.5 * v) + 0.5


def _silu(v):
    h = 0.5 * v
    return h * jnp.tanh(h) + h


def _sqrt_nonneg(v):
    return v * lax.rsqrt(jnp.maximum(v, 1e-30))


def _const_spec(shape, index=None):
    index = (0,) * len(shape) if index is None else tuple(index)
    return pl.BlockSpec(shape, lambda *_: index, pipeline_mode=pl.Buffered(1))


def _layer_vec(l):
    return _const_spec((None, 1, D), (l, 0, 0))


def _params(n_grid):
    return pltpu.CompilerParams(
        dimension_semantics=("arbitrary",) * n_grid, vmem_limit_bytes=VMEM_LIMIT)


def _mod_kernel(cv_ref, w_ref, b_ref, o_ref):
    s = _silu(cv_ref[...])
    o_ref[...] = jnp.dot(s, w_ref[...], precision=lax.Precision.HIGHEST,
                         preferred_element_type=F32) + b_ref[...]


def _modulation(cv, w_ada, b_ada):
    n_cols = N_MOD * D
    return pl.pallas_call(
        _mod_kernel,
        grid=(DEPTH, n_cols // MOD_TN),
        in_specs=[
            pl.BlockSpec((MOD_ROWS, D), lambda l, n: (0, 0)),
            pl.BlockSpec((None, D, MOD_TN), lambda l, n: (l, 0, n)),
            pl.BlockSpec((None, 1, MOD_TN), lambda l, n: (l, 0, n)),
        ],
        out_specs=pl.BlockSpec((None, MOD_ROWS, MOD_TN), lambda l, n: (l, 0, n)),
        out_shape=jax.ShapeDtypeStruct((DEPTH, MOD_ROWS, n_cols), F32),
        compiler_params=_params(2),
        name="adaln_mod",
    )(cv, w_ada, b_ada.reshape(DEPTH, 1, n_cols))


def _ffn_kernel(x_ref, mod_ref, wup_ref, wdn_ref, lng_ref, lnb_ref, o_ref, act_ref, *, k):
    x = x_ref[...]
    shift = mod_ref[3 * k:3 * k + 1, :]
    scale = mod_ref[3 * k + 1:3 * k + 2, :]
    gate = mod_ref[3 * k + 2:3 * k + 3, :]
    h = (x * (1.0 + scale) + shift).astype(BF16)
    for c in range(D_FF // FFN_CK):
        lo = c * FFN_CK
        gt = _dot(h, wup_ref[:, lo:lo + FFN_CK])
        up = _dot(h, wup_ref[:, D_FF + lo:D_FF + lo + FFN_CK])
        act_ref[:, lo:lo + FFN_CK] = (_silu(gt) * up).astype(BF16)
    y = _dot(act_ref[...], wdn_ref[...])
    o_ref[...] = _ln(ALPHA * x + (0.5 * gate) * y, lng_ref[...], lnb_ref[...])


def _ffn(x2, w, *, l, half, rows_per_mod, mod_base):
    n = x2.shape[0]
    k = 2 * half
    tiles_per_mod = rows_per_mod // FFN_TM
    return pl.pallas_call(
        functools.partial(_ffn_kernel, k=k),
        grid=(n // FFN_TM,),
        in_specs=[
            pl.BlockSpec((FFN_TM, D), lambda j: (j, 0)),
            pl.BlockSpec((None, None, N_MOD, D),
                         lambda j: (l, mod_base + j // tiles_per_mod, 0, 0)),
            _const_spec((None, None, D, 2 * D_FF), (l, half, 0, 0)),
            _const_spec((None, None, D_FF, D), (l, half, 0, 0)),
            _const_spec((None, None, 1, D), (l, k, 0, 0)),
            _const_spec((None, None, 1, D), (l, k, 0, 0)),
        ],
        out_specs=pl.BlockSpec((FFN_TM, D), lambda j: (j, 0)),
        out_shape=jax.ShapeDtypeStruct((n, D), F32),
        scratch_shapes=[pltpu.VMEM((FFN_TM, D_FF), BF16)],
        compiler_params=_params(1),
        name="ffn",
    )(x2, w["mod"], w["wup"], w["wdn"], w["lng"], w["lnb"])


def _lru_coeffs(xcv_ref, wg_ref, bg_ref, lam_ref, a_ref, b_ref, d):
    nsp = -LRU_C * jax.nn.softplus(-lam_ref[d:d + 1, :])
    for hd in range(C_HEADS):
        sl = slice(hd * C_HEAD_DIM, (hd + 1) * C_HEAD_DIM)
        xh = xcv_ref[:, sl]
        g = _dot(xh.astype(BF16), wg_ref[d, hd])
        r = _sigmoid(g[:, :C_HEAD_DIM] + bg_ref[2 * d:2 * d + 1, sl])
        i = _sigmoid(g[:, C_HEAD_DIM:] + bg_ref[2 * d + 1:2 * d + 2, sl])
        a = jnp.exp(nsp[:, sl] * r)
        a_ref[:, sl] = a
        b_ref[:, sl] = _sqrt_nonneg(1.0 - a * a) * (i * xh)


def _scan_rows(a_ref, b_ref, out_ref, h_init, n_rows, reverse, accumulate):
    def body(s, h):
        t = n_rows - 1 - s if reverse else s
        h = a_ref[pl.ds(t, 1), :] * h + b_ref[pl.ds(t, 1), :]
        if accumulate:
            out_ref[pl.ds(t, 1), :] = out_ref[pl.ds(t, 1), :] + h
        else:
            out_ref[pl.ds(t, 1), :] = h
        return h
    return lax.fori_loop(0, n_rows, body, h_init, unroll=8)


def _lru_specs(l):
    return [
        _const_spec((None, 2, C_HEADS, C_HEAD_DIM, 2 * C_HEAD_DIM), (l, 0, 0, 0, 0)),
        _const_spec((None, 4, D), (l, 0, 0)),
        _const_spec((None, 2, D), (l, 0, 0)),
    ]


def _scan_kernel(xl_ref, xc_ref, xr_ref, mod_ref, wxc_ref, bxc_ref, wcv_ref, bcv_ref,
                 wg_ref, bg_ref, lam_ref, h0f_ref, h0b_ref,
                 xcv_ref, hs_ref, endf_ref, endb_ref,
                 zx_ref, a_ref, b_ref, carry_ref, *, tile, n_tiles, dirs, reverse):
    j = pl.program_id(1)
    jj = n_tiles - 1 - j if reverse else j
    shift = mod_ref[3:4, :]
    scale = mod_ref[4:5, :]
    xe = jnp.concatenate([xl_ref[...], xc_ref[...], xr_ref[...]], axis=0)
    he = (xe * (1.0 + scale) + shift).astype(BF16)
    zx = _dot(he, wxc_ref[...]) + bxc_ref[...]
    row = lax.broadcasted_iota(jnp.int32, (tile + 2 * HALO_C, 1), 0)
    valid = jnp.logical_and(jnp.logical_or(row >= HALO_C, jj > 0),
                            jnp.logical_or(row < tile + HALO_C, jj < n_tiles - 1))
    zx_ref[...] = jnp.where(valid, zx, 0.0)
    acc = bcv_ref[...] + wcv_ref[0:1, :] * zx_ref[pl.ds(HALO_C - 2, tile), :]
    for k in range(1, CONV_C):
        acc = acc + wcv_ref[k:k + 1, :] * zx_ref[pl.ds(HALO_C - 2 + k, tile), :]
    xcv_ref[...] = acc

    for d, end_ref in ((0, endf_ref), (1, endb_ref)):
        if d not in dirs:
            end_ref[...] = jnp.zeros_like(end_ref)
    first = True
    for d in dirs:
        _lru_coeffs(xcv_ref, wg_ref, bg_ref, lam_ref, a_ref, b_ref, d)
        h0_ref = h0b_ref if d == 1 else h0f_ref
        end_ref = endb_ref if d == 1 else endf_ref

        @pl.when(j == 0)
        def _():
            carry_ref[d:d + 1, :] = h0_ref[...]

        h = _scan_rows(a_ref, b_ref, hs_ref, carry_ref[d:d + 1, :], tile,
                       reverse=(d == 1), accumulate=not first)
        carry_ref[d:d + 1, :] = h
        end_ref[...] = h
        first = False


def _scan_call(x3, w, h0f, h0b, *, l, mod_base, tile, dirs, reverse):
    bsz, s, _ = x3.shape
    n_tiles = s // tile
    hb = tile // HALO_C
    n_hblk = s // HALO_C

    def ctr(b, j):
        return (b, n_tiles - 1 - j if reverse else j, 0)

    def left(b, j):
        jj = n_tiles - 1 - j if reverse else j
        return (b, jnp.maximum(jj * hb - 1, 0), 0)

    def right(b, j):
        jj = n_tiles - 1 - j if reverse else j
        return (b, jnp.minimum((jj + 1) * hb, n_hblk - 1), 0)

    vec = pl.BlockSpec((None, 1, D), lambda b, j: (b, 0, 0))
    return pl.pallas_call(
        functools.partial(_scan_kernel, tile=tile, n_tiles=n_tiles, dirs=dirs, reverse=reverse),
        grid=(bsz, n_tiles),
        in_specs=[
            pl.BlockSpec((None, HALO_C, D), left),
            pl.BlockSpec((None, tile, D), ctr),
            pl.BlockSpec((None, HALO_C, D), right),
            pl.BlockSpec((None, None, N_MOD, D), lambda b, j: (l, mod_base(b), 0, 0)),
            _const_spec((None, D, D), (l, 0, 4)),
            _const_spec((None, 1, D), (l, 0, 4)),
            _const_spec((None, CONV_C, D), (l, 0, 0)),
            _layer_vec(l),
            *_lru_specs(l),
            vec, vec,
        ],
        out_specs=[
            pl.BlockSpec((None, tile, D), ctr),
            pl.BlockSpec((None, tile, D), ctr),
            vec, vec,
        ],
        out_shape=[
            jax.ShapeDtypeStruct((bsz, s, D), F32),
            jax.ShapeDtypeStruct((bsz, s, D), F32),
            jax.ShapeDtypeStruct((bsz, 1, D), F32),
            jax.ShapeDtypeStruct((bsz, 1, D), F32),
        ],
        scratch_shapes=[
            pltpu.VMEM((tile + 2 * HALO_C, D), F32),
            pltpu.VMEM((tile, D), F32),
            pltpu.VMEM((tile, D), F32),
            pltpu.VMEM((2, D), F32),
        ],
        compiler_params=_params(2),
        name="lru_scan",
    )(x3, x3, x3, w["mod"], w["win"], w["bin"], w["wcv"], w["bcv"],
      w["wg"], w["bg"], w["lam"], h0f, h0b)


def _conv31(hb_ref, wdw_ref, n_rows):
    groups = {}
    for k in range(CONV_W):
        off = k + HALO_B - CONV_HALF
        groups.setdefault(off % SUBLANES, []).append((k, off - off % SUBLANES))
    acc = None
    for s, taps in sorted(groups.items()):
        n = n_rows if s == 0 else n_rows + SUBLANES
        g = None
        for k, base in taps:
            term = wdw_ref[k:k + 1, :] * hb_ref[pl.ds(base, n), :]
            g = term if g is None else g + term
        part = g if s == 0 else g[s:s + n_rows]
        acc = part if acc is None else acc + part
    return acc


def _branch_kernel(xl_ref, xc_ref, xr_ref, mod_ref, win_ref, bin_ref,
                   alg_ref, alb_ref, ws_ref, bs_ref, wdw_ref, bdw_ref, blg_ref, blb_ref, wbr_ref,
                   part_ref, gcg_ref, sgc_ref,
                   hb_ref, ya_ref, *, tile, n_tiles):
    j = pl.program_id(1)
    shift = mod_ref[3:4, :]
    scale = mod_ref[4:5, :]
    xe = jnp.concatenate([xl_ref[...], xc_ref[...], xr_ref[...]], axis=0)
    he = (xe * (1.0 + scale) + shift).astype(BF16)
    hc = he[HALO_B:HALO_B + tile]

    a_b = _dot(he, win_ref[:, 2 * D:3 * D]) + bin_ref[:, 2 * D:3 * D]
    g_b = _dot(he, win_ref[:, 3 * D:4 * D]) + bin_ref[:, 3 * D:4 * D]
    row = lax.broadcasted_iota(jnp.int32, (tile + 2 * HALO_B, 1), 0)
    valid = jnp.logical_and(jnp.logical_or(row >= HALO_B, j > 0),
                            jnp.logical_or(row < tile + HALO_B, j < n_tiles - 1))
    hb_ref[...] = jnp.where(valid, a_b * _sigmoid(g_b), 0.0)
    acc = _conv31(hb_ref, wdw_ref, tile) + bdw_ref[...]
    y_b = _silu(_ln(acc, blg_ref[...], blb_ref[...])).astype(BF16)
    p_b = _dot(y_b, wbr_ref[1])

    u_a = _dot(hc, win_ref[:, 0:D]) + bin_ref[:, 0:D]
    v_a = _dot(hc, win_ref[:, D:2 * D]) + bin_ref[:, D:2 * D]
    vn = _ln(v_a, alg_ref[...], alb_ref[...]).astype(BF16)
    for n in range(tile // CHUNK):
        rows = slice(n * CHUNK, (n + 1) * CHUNK)
        mixed = jnp.concatenate(
            [_dot(ws_ref[g], vn[rows, g * A_GROUP_W:(g + 1) * A_GROUP_W]) for g in range(A_GROUPS)],
            axis=1)
        ya_ref[rows, :] = (u_a[rows, :] * (mixed + bs_ref[...])).astype(BF16)
    p_a = _dot(ya_ref[...], wbr_ref[0])

    sg_a = _sigmoid(_dot(hc, win_ref[:, 6 * D:7 * D]) + bin_ref[:, 6 * D:7 * D])
    sg_b = _sigmoid(_dot(hc, win_ref[:, 7 * D:8 * D]) + bin_ref[:, 7 * D:8 * D])
    part_ref[...] = sg_a * p_a + sg_b * p_b
    sgc_ref[...] = _sigmoid(_dot(hc, win_ref[:, 8 * D:9 * D]) + bin_ref[:, 8 * D:9 * D])
    gcg_ref[...] = jax.nn.gelu(_dot(hc, win_ref[:, 5 * D:6 * D]) + bin_ref[:, 5 * D:6 * D])


def _branch_call(x3, w, *, l, mod_base, tile):
    bsz, s, _ = x3.shape
    n_tiles = s // tile
    hb = tile // HALO_B
    n_hblk = s // HALO_B
    ctr = pl.BlockSpec((None, tile, D), lambda b, j: (b, j, 0))
    out = jax.ShapeDtypeStruct((bsz, s, D), F32)
    return pl.pallas_call(
        functools.partial(_branch_kernel, tile=tile, n_tiles=n_tiles),
        grid=(bsz, n_tiles),
        in_specs=[
            pl.BlockSpec((None, HALO_B, D), lambda b, j: (b, jnp.maximum(j * hb - 1, 0), 0)),
            ctr,
            pl.BlockSpec((None, HALO_B, D), lambda b, j: (b, jnp.minimum((j + 1) * hb, n_hblk - 1), 0)),
            pl.BlockSpec((None, None, N_MOD, D), lambda b, j: (l, mod_base(b), 0, 0)),
            _const_spec((None, D, N_MOD * D), (l, 0, 0)),
            _const_spec((None, 1, N_MOD * D), (l, 0, 0)),
            _layer_vec(l),
            _layer_vec(l),
            _const_spec((None, A_GROUPS, CHUNK, CHUNK), (l, 0, 0, 0)),
            _const_spec((None, CHUNK, D), (l, 0, 0)),
            _const_spec((None, CONV_W, D), (l, 0, 0)),
            _layer_vec(l),
            _layer_vec(l),
            _layer_vec(l),
            _const_spec((None, 2, D, D), (l, 0, 0, 0)),
        ],
        out_specs=[ctr, ctr, ctr],
        out_shape=[out, out, out],
        scratch_shapes=[
            pltpu.VMEM((tile + 2 * HALO_B, D), F32),
            pltpu.VMEM((tile, D), BF16),
        ],
        compiler_params=_params(2),
        name="mixer_branches",
    )(x3, x3, x3, w["mod"], w["win"], w["bin"], w["alg"], w["alb"], w["ws"], w["bs"],
      w["wdw"], w["bdw"], w["blg"], w["blb"], w["wbr"])


def _merge_tail(x_ref, r, part_ref, gcg_ref, sgc_ref, mod_ref, wbc_ref, wo_ref, lng_ref, lnb_ref,
                o_ref):
    y_c = (r * gcg_ref[...]).astype(BF16)
    m = part_ref[...] + sgc_ref[...] * _dot(y_c, wbc_ref[...])
    y = _dot(m.astype(BF16), wo_ref[...])
    gate = mod_ref[5:6, :]
    o_ref[...] = _ln(ALPHA * x_ref[...] + gate * y, lng_ref[...], lnb_ref[...])


def _merge_scan_kernel(x_ref, xcv_ref, hs_ref, part_ref, gcg_ref, sgc_ref, mod_ref,
                       wg_ref, bg_ref, lam_ref, h0f_ref, wbc_ref, wo_ref, lng_ref, lnb_ref,
                       o_ref, a_ref, b_ref, hf_ref, carry_ref, *, tile):
    _lru_coeffs(xcv_ref, wg_ref, bg_ref, lam_ref, a_ref, b_ref, 0)

    @pl.when(pl.program_id(1) == 0)
    def _():
        carry_ref[...] = h0f_ref[...]

    carry_ref[...] = _scan_rows(a_ref, b_ref, hf_ref, carry_ref[...], tile,
                                reverse=False, accumulate=False)
    _merge_tail(x_ref, hf_ref[...] + hs_ref[...], part_ref, gcg_ref, sgc_ref, mod_ref,
                wbc_ref, wo_ref, lng_ref, lnb_ref, o_ref)


def _merge_kernel(x_ref, hs_ref, part_ref, gcg_ref, sgc_ref, mod_ref,
                  wbc_ref, wo_ref, lng_ref, lnb_ref, o_ref):
    _merge_tail(x_ref, hs_ref[...], part_ref, gcg_ref, sgc_ref, mod_ref,
                wbc_ref, wo_ref, lng_ref, lnb_ref, o_ref)


def _merge_call(x3, hs, part, gcg, sgc, w, *, l, mod_base, tile, xcv=None, h0f=None):
    bsz, s, _ = x3.shape
    ctr = pl.BlockSpec((None, tile, D), lambda b, j: (b, j, 0))
    mod_spec = pl.BlockSpec((None, None, N_MOD, D), lambda b, j: (l, mod_base(b), 0, 0))
    tail_specs = [
        _const_spec((None, None, D, D), (l, 2, 0, 0)),
        _const_spec((None, D, D), (l, 0, 0)),
        _const_spec((None, None, 1, D), (l, 1, 0, 0)),
        _const_spec((None, None, 1, D), (l, 1, 0, 0)),
    ]
    tail_args = (w["wbr"], w["wo"], w["lng"], w["lnb"])
    if xcv is None:
        body = _merge_kernel
        in_specs = [ctr, ctr, ctr, ctr, ctr, mod_spec, *tail_specs]
        args = (x3, hs, part, gcg, sgc, w["mod"], *tail_args)
        scratch = []
    else:
        body = functools.partial(_merge_scan_kernel, tile=tile)
        in_specs = [ctr, ctr, ctr, ctr, ctr, ctr, mod_spec, *_lru_specs(l),
                    pl.BlockSpec((None, 1, D), lambda b, j: (b, 0, 0)), *tail_specs]
        args = (x3, xcv, hs, part, gcg, sgc, w["mod"], w["wg"], w["bg"], w["lam"], h0f, *tail_args)
        scratch = [pltpu.VMEM((tile, D), F32), pltpu.VMEM((tile, D), F32),
                   pltpu.VMEM((tile, D), F32), pltpu.VMEM((1, D), F32)]
    return pl.pallas_call(
        body,
        grid=(bsz, s // tile),
        in_specs=in_specs,
        out_specs=ctr,
        out_shape=jax.ShapeDtypeStruct((bsz, s, D), F32),
        scratch_shapes=scratch,
        compiler_params=_params(2),
        name="mixer_merge",
    )(*args)


def kernel(x, c, ctx, c_ctx, w_ada, b_ada, ln_post_g, ln_post_b, ffn_w_up, ffn_w_down,
           w_in, b_in, a_ln_g, a_ln_b, a_w_s, a_b_s, b_w_dw, b_b_dw, b_ln_g, b_ln_b,
           c_w_conv, c_b_conv, c_w_a, c_b_a, c_w_x, c_b_x, c_lam, w_br, w_o):
    bsz, seq, _ = x.shape
    ctx_len = ctx.shape[1]

    cv = jnp.concatenate(
        [c_ctx[None, :], c, jnp.zeros((MOD_ROWS - 1 - bsz, D), F32)], axis=0)
    row = lambda v: v.reshape(DEPTH, 1, -1)
    w = {
        "mod": _modulation(cv, w_ada, b_ada).reshape(DEPTH, MOD_ROWS, N_MOD, D),
        "wup": ffn_w_up.astype(BF16),
        "wdn": ffn_w_down.astype(BF16),
        "lng": ln_post_g.reshape(DEPTH, 3, 1, D),
        "lnb": ln_post_b.reshape(DEPTH, 3, 1, D),
        "win": w_in.astype(BF16),
        "bin": row(b_in),
        "alg": row(a_ln_g), "alb": row(a_ln_b),
        "ws": a_w_s.astype(BF16),
        "bs": jnp.repeat(jnp.swapaxes(a_b_s, 1, 2), A_GROUP_W, axis=2),
        "wdw": b_w_dw, "bdw": row(b_b_dw), "blg": row(b_ln_g), "blb": row(b_ln_b),
        "wcv": c_w_conv, "bcv": row(c_b_conv),
        "wg": jnp.concatenate([c_w_a, c_w_x], axis=-1).astype(BF16),
        "bg": jnp.stack([c_b_a[:, 0], c_b_x[:, 0], c_b_a[:, 1], c_b_x[:, 1]], axis=1),
        "lam": c_lam,
        "wbr": w_br.astype(BF16),
        "wo": w_o.astype(BF16),
    }
    lat_mod = lambda b: 1 + b
    ctx_mod = lambda b: 0
    zeros_h = jnp.zeros((bsz, 1, D), F32)

    def ffn_lat(v, l, half):
        return _ffn(v.reshape(bsz * seq, D), w, l=l, half=half, rows_per_mod=seq,
                    mod_base=1).reshape(bsz, seq, D)

    def ffn_ctx(v, l, half):
        return _ffn(v.reshape(bsz * ctx_len, D), w, l=l, half=half, rows_per_mod=bsz * ctx_len,
                    mod_base=0).reshape(bsz, ctx_len, D)

    for l in range(DEPTH):
        ctx_out = l < DEPTH - 1
        x = ffn_lat(x, l, 0)
        ctx = ffn_ctx(ctx, l, 0)

        _, hs_ctx, end_f, end_b = _scan_call(
            ctx, w, zeros_h, zeros_h, l=l, mod_base=ctx_mod, tile=CTX_TILE, dirs=(0, 1),
            reverse=False)
        xcv_lat, hb_lat, _, _ = _scan_call(
            x, w, end_f, end_b, l=l, mod_base=lat_mod, tile=LAT_TILE, dirs=(1,), reverse=True)

        part, gcg, sgc = _branch_call(x, w, l=l, mod_base=lat_mod, tile=LAT_TILE)
        x = _merge_call(x, hb_lat, part, gcg, sgc, w, l=l, mod_base=lat_mod, tile=LAT_TILE,
                        xcv=xcv_lat, h0f=end_f)
        x = ffn_lat(x, l, 1)

        if ctx_out:
            part, gcg, sgc = _branch_call(ctx, w, l=l, mod_base=ctx_mod, tile=CTX_TILE)
            ctx = _merge_call(ctx, hs_ctx, part, gcg, sgc, w, l=l, mod_base=ctx_mod,
                              tile=CTX_TILE)
            ctx = ffn_ctx(ctx, l, 1)
    return x
```

```python
import functools

import jax
import jax.numpy as jnp
from jax import lax
from jax.experimental import pallas as pl
from jax.experimental.pallas import tpu as pltpu

D = 1024
DEPTH = 4
CHUNK = 128
A_GROUPS = 4
A_GROUP_W = D // A_GROUPS
CONV_W = 31
CONV_HALF = CONV_W // 2
C_HEADS = 8
C_HEAD_DIM = D // C_HEADS
CONV_C = 4
LRU_C = 8.0
D_FF = 2816
N_MOD = 9
ALPHA = (2.0 * DEPTH) ** 0.25
LN_EPS = 1e-5
LOG2_E = 1.4426950408889634

SUBLANES = 8
MOD_ROWS = 8
MOD_TN = 1152
FFN_TM = 1024
FFN_CK = 256
CW = A_GROUP_W
HALO_B = 16
HALO_C = 8
LAT_TILE = 512
CTX_TILE = 256
VMEM_LIMIT = 60 * 1024 * 1024

F32 = jnp.float32
BF16 = jnp.bfloat16


def _dot(a, b):
    return jnp.dot(a, b, preferred_element_type=F32)


def _ln(v, g, b):
    mu = jnp.mean(v, axis=-1, keepdims=True)
    c = v - mu
    var = jnp.mean(c * c, axis=-1, keepdims=True)
    return c * lax.rsqrt(var + LN_EPS) * g + b


def _sigmoid(v):
    return 0.5 * jnp.tanh(0.5 * v) + 0.5


def _silu(v):
    h = 0.5 * v
    return h * jnp.tanh(h) + h


def _gelu(v):
    c0 = 0.7978845608028654
    h = 0.5 * v
    return h + h * jnp.tanh(v * (c0 + (c0 * 0.044715) * (v * v)))


def _sqrt_nonneg(v):
    return v * lax.rsqrt(jnp.maximum(v, 1e-30))


def _const_spec(shape, index=None):
    index = (0,) * len(shape) if index is None else tuple(index)
    return pl.BlockSpec(shape, lambda *_: index, pipeline_mode=pl.Buffered(1))


def _layer_vec(l):
    return _const_spec((None, 1, D), (l, 0, 0))


def _params(n_grid):
    return pltpu.CompilerParams(
        dimension_semantics=("arbitrary",) * n_grid, vmem_limit_bytes=VMEM_LIMIT)


def _mod_kernel(cv_ref, w_ref, b_ref, o_ref):
    s = _silu(cv_ref[...])
    o_ref[...] = jnp.dot(s, w_ref[...], precision=lax.Precision.HIGHEST,
                         preferred_element_type=F32) + b_ref[...]


def _modulation(cv, w_ada, b_ada):
    n_cols = N_MOD * D
    return pl.pallas_call(
        _mod_kernel,
        grid=(DEPTH, n_cols // MOD_TN),
        in_specs=[
            pl.BlockSpec((MOD_ROWS, D), lambda l, n: (0, 0)),
            pl.BlockSpec((None, D, MOD_TN), lambda l, n: (l, 0, n)),
            pl.BlockSpec((None, 1, MOD_TN), lambda l, n: (l, 0, n)),
        ],
        out_specs=pl.BlockSpec((None, MOD_ROWS, MOD_TN), lambda l, n: (l, 0, n)),
        out_shape=jax.ShapeDtypeStruct((DEPTH, MOD_ROWS, n_cols), F32),
        compiler_params=_params(2),
        name="adaln_mod",
    )(cv, w_ada, b_ada.reshape(DEPTH, 1, n_cols))


def _ffn_kernel(x_ref, mod_ref, wup_ref, wdn_ref, lng_ref, lnb_ref, o_ref, act_ref, *, k):
    x = x_ref[...]
    shift = mod_ref[3 * k:3 * k + 1, :]
    scale = mod_ref[3 * k + 1:3 * k + 2, :]
    gate = mod_ref[3 * k + 2:3 * k + 3, :]
    h = (x * (1.0 + scale) + shift).astype(BF16)
    for c in range(D_FF // FFN_CK):
        lo = c * FFN_CK
        gt = _dot(h, wup_ref[:, lo:lo + FFN_CK])
        up = _dot(h, wup_ref[:, D_FF + lo:D_FF + lo + FFN_CK])
        act_ref[:, lo:lo + FFN_CK] = (_silu(gt) * up).astype(BF16)
    half_gate = 0.5 * gate
    res = jnp.concatenate(
        [ALPHA * x_ref[:, lo:lo + CW] + half_gate[:, lo:lo + CW] * _dot(act_ref[...], wdn_ref[:, lo:lo + CW])
         for lo in range(0, D, CW)], axis=1)
    o_ref[...] = _ln(res, lng_ref[...], lnb_ref[...])


def _ffn(x2, w, *, l, half, rows_per_mod, mod_base):
    n = x2.shape[0]
    k = 2 * half
    tiles_per_mod = rows_per_mod // FFN_TM
    return pl.pallas_call(
        functools.partial(_ffn_kernel, k=k),
        grid=(n // FFN_TM,),
        in_specs=[
            pl.BlockSpec((FFN_TM, D), lambda j: (j, 0)),
            pl.BlockSpec((None, None, N_MOD, D),
                         lambda j: (l, mod_base + j // tiles_per_mod, 0, 0)),
            _const_spec((None, None, D, 2 * D_FF), (l, half, 0, 0)),
            _const_spec((None, None, D_FF, D), (l, half, 0, 0)),
            _const_spec((None, None, 1, D), (l, k, 0, 0)),
            _const_spec((None, None, 1, D), (l, k, 0, 0)),
        ],
        out_specs=pl.BlockSpec((FFN_TM, D), lambda j: (j, 0)),
        out_shape=jax.ShapeDtypeStruct((n, D), F32),
        scratch_shapes=[pltpu.VMEM((FFN_TM, D_FF), BF16)],
        compiler_params=_params(1),
        name="ffn",
    )(x2, w["mod"], w["wup"], w["wdn"], w["lng"], w["lnb"])


def _lru_coeffs(xcv_ref, wg_ref, bg_ref, lam_ref, a_ref, b_ref, d):
    k = (-0.5 * LRU_C * LOG2_E) * jax.nn.softplus(-lam_ref[d:d + 1, :])
    half_br = 0.5 * bg_ref[2 * d:2 * d + 1, :]
    half_bi = 0.5 * bg_ref[2 * d + 1:2 * d + 2, :]
    for hd in range(C_HEADS):
        sl = slice(hd * C_HEAD_DIM, (hd + 1) * C_HEAD_DIM)
        xh = xcv_ref[:, sl]
        g = _dot(xh.astype(BF16), wg_ref[d, hd])
        t_r = jnp.tanh(g[:, :C_HEAD_DIM] + half_br[:, sl])
        t_i = jnp.tanh(g[:, C_HEAD_DIM:] + half_bi[:, sl])
        a = jnp.exp2(k[:, sl] * t_r + k[:, sl])
        hx = 0.5 * xh
        a_ref[:, sl] = a
        b_ref[:, sl] = _sqrt_nonneg(1.0 - a * a) * (hx * t_i + hx)


def _scan_rows(a_ref, b_ref, out_ref, h_init, n_rows, reverse, accumulate):
    def body(s, h):
        t = n_rows - 1 - s if reverse else s
        h = a_ref[pl.ds(t, 1), :] * h + b_ref[pl.ds(t, 1), :]
        if accumulate:
            out_ref[pl.ds(t, 1), :] = out_ref[pl.ds(t, 1), :] + h
        else:
            out_ref[pl.ds(t, 1), :] = h
        return h
    return lax.fori_loop(0, n_rows, body, h_init, unroll=8)


def _lru_specs(l):
    return [
        _const_spec((None, 2, C_HEADS, C_HEAD_DIM, 2 * C_HEAD_DIM), (l, 0, 0, 0, 0)),
        _const_spec((None, 4, D), (l, 0, 0)),
        _const_spec((None, 2, D), (l, 0, 0)),
    ]


def _scan_kernel(xl_ref, xc_ref, xr_ref, mod_ref, wxc_ref, bxc_ref, wcv_ref, bcv_ref,
                 wg_ref, bg_ref, lam_ref, h0f_ref, h0b_ref,
                 xcv_ref, hs_ref, endf_ref, endb_ref,
                 zx_ref, a_ref, b_ref, carry_ref, *, tile, n_tiles, dirs, reverse):
    j = pl.program_id(1)
    jj = n_tiles - 1 - j if reverse else j
    shift = mod_ref[3:4, :]
    scale = mod_ref[4:5, :]
    xe = jnp.concatenate([xl_ref[...], xc_ref[...], xr_ref[...]], axis=0)
    he = (xe * (1.0 + scale) + shift).astype(BF16)
    zx = _dot(he, wxc_ref[...]) + bxc_ref[...]
    row = lax.broadcasted_iota(jnp.int32, (tile + 2 * HALO_C, 1), 0)
    valid = jnp.logical_and(jnp.logical_or(row >= HALO_C, jj > 0),
                            jnp.logical_or(row < tile + HALO_C, jj < n_tiles - 1))
    zx_ref[...] = jnp.where(valid, zx, 0.0)
    acc = bcv_ref[...] + wcv_ref[0:1, :] * zx_ref[pl.ds(HALO_C - 2, tile), :]
    for k in range(1, CONV_C):
        acc = acc + wcv_ref[k:k + 1, :] * zx_ref[pl.ds(HALO_C - 2 + k, tile), :]
    xcv_ref[...] = acc

    for d, end_ref in ((0, endf_ref), (1, endb_ref)):
        if d not in dirs:
            end_ref[...] = jnp.zeros_like(end_ref)
    first = True
    for d in dirs:
        _lru_coeffs(xcv_ref, wg_ref, bg_ref, lam_ref, a_ref, b_ref, d)
        h0_ref = h0b_ref if d == 1 else h0f_ref
        end_ref = endb_ref if d == 1 else endf_ref

        @pl.when(j == 0)
        def _():
            carry_ref[d:d + 1, :] = h0_ref[...]

        h = _scan_rows(a_ref, b_ref, hs_ref, carry_ref[d:d + 1, :], tile,
                       reverse=(d == 1), accumulate=not first)
        carry_ref[d:d + 1, :] = h
        end_ref[...] = h
        first = False


def _scan_call(x3, w, h0f, h0b, *, l, mod_base, tile, dirs, reverse):
    bsz, s, _ = x3.shape
    n_tiles = s // tile
    hb = tile // HALO_C
    n_hblk = s // HALO_C

    def ctr(b, j):
        return (b, n_tiles - 1 - j if reverse else j, 0)

    def left(b, j):
        jj = n_tiles - 1 - j if reverse else j
        return (b, jnp.maximum(jj * hb - 1, 0), 0)

    def right(b, j):
        jj = n_tiles - 1 - j if reverse else j
        return (b, jnp.minimum((jj + 1) * hb, n_hblk - 1), 0)

    vec = pl.BlockSpec((None, 1, D), lambda b, j: (b, 0, 0))
    return pl.pallas_call(
        functools.partial(_scan_kernel, tile=tile, n_tiles=n_tiles, dirs=dirs, reverse=reverse),
        grid=(bsz, n_tiles),
        in_specs=[
            pl.BlockSpec((None, HALO_C, D), left),
            pl.BlockSpec((None, tile, D), ctr),
            pl.BlockSpec((None, HALO_C, D), right),
            pl.BlockSpec((None, None, N_MOD, D), lambda b, j: (l, mod_base(b), 0, 0)),
            _const_spec((None, D, D), (l, 0, 4)),
            _const_spec((None, 1, D), (l, 0, 4)),
            _const_spec((None, CONV_C, D), (l, 0, 0)),
            _layer_vec(l),
            *_lru_specs(l),
            vec, vec,
        ],
        out_specs=[
            pl.BlockSpec((None, tile, D), ctr),
            pl.BlockSpec((None, tile, D), ctr),
            vec, vec,
        ],
        out_shape=[
            jax.ShapeDtypeStruct((bsz, s, D), F32),
            jax.ShapeDtypeStruct((bsz, s, D), F32),
            jax.ShapeDtypeStruct((bsz, 1, D), F32),
            jax.ShapeDtypeStruct((bsz, 1, D), F32),
        ],
        scratch_shapes=[
            pltpu.VMEM((tile + 2 * HALO_C, D), F32),
            pltpu.VMEM((tile, D), F32),
            pltpu.VMEM((tile, D), F32),
            pltpu.VMEM((2, D), F32),
        ],
        compiler_params=_params(2),
        name="lru_scan",
    )(x3, x3, x3, w["mod"], w["win"], w["bin"], w["wcv"], w["bcv"],
      w["wg"], w["bg"], w["lam"], h0f, h0b)


def _conv31(hb_ref, wdw_ref, n_rows):
    groups = {}
    for k in range(CONV_W):
        off = k + HALO_B - CONV_HALF
        groups.setdefault(off % SUBLANES, []).append((k, off - off % SUBLANES))
    acc = None
    for s, taps in sorted(groups.items()):
        n = n_rows if s == 0 else n_rows + SUBLANES
        g = None
        for k, base in taps:
            term = wdw_ref[k:k + 1, :] * hb_ref[pl.ds(base, n), :]
            g = term if g is None else g + term
        part = g if s == 0 else g[s:s + n_rows]
        acc = part if acc is None else acc + part
    return acc


def _branch_kernel(xl_ref, xc_ref, xr_ref, mod_ref, win_ref, bin_ref,
                   alg_ref, alb_ref, ws_ref, bs_ref, wdw_ref, bdw_ref, blg_ref, blb_ref, wbr_ref,
                   part_ref, gcg_ref, sgc_ref,
                   hb_ref, yb_ref, v_ref, vn_ref, ya_ref, *, tile, n_tiles):
    j = pl.program_id(1)
    shift = mod_ref[3:4, :]
    scale = mod_ref[4:5, :]
    xe = jnp.concatenate([xl_ref[...], xc_ref[...], xr_ref[...]], axis=0)
    he = (xe * (1.0 + scale) + shift).astype(BF16)
    hc = he[HALO_B:HALO_B + tile]

    def proj(lhs, group, cols):
        lo = group * D + cols.start
        return _dot(lhs, win_ref[:, lo:lo + CW]) + bin_ref[:, lo:lo + CW]

    chunks = [slice(c * CW, (c + 1) * CW) for c in range(D // CW)]

    row = lax.broadcasted_iota(jnp.int32, (tile + 2 * HALO_B, 1), 0)
    valid = jnp.logical_and(jnp.logical_or(row >= HALO_B, j > 0),
                            jnp.logical_or(row < tile + HALO_B, j < n_tiles - 1))
    for cols in chunks:
        hb_ref[:, cols] = jnp.where(valid, proj(he, 2, cols) * _sigmoid(proj(he, 3, cols)), 0.0)
    acc = _conv31(hb_ref, wdw_ref, tile) + bdw_ref[...]
    yb_ref[...] = _silu(_ln(acc, blg_ref[...], blb_ref[...])).astype(BF16)

    for cols in chunks:
        v_ref[:, cols] = proj(hc, 1, cols)
    vn_ref[...] = _ln(v_ref[...], alg_ref[...], alb_ref[...]).astype(BF16)
    for g, cols in enumerate(chunks):
        u_a = proj(hc, 0, cols)
        for n in range(tile // CHUNK):
            rows = slice(n * CHUNK, (n + 1) * CHUNK)
            mixed = _dot(ws_ref[g], vn_ref[rows, cols]) + bs_ref[:, cols]
            ya_ref[rows, cols] = (u_a[rows, :] * mixed).astype(BF16)

    for cols in chunks:
        p_a = _dot(ya_ref[...], wbr_ref[0, :, cols])
        p_b = _dot(yb_ref[...], wbr_ref[1, :, cols])
        part_ref[:, cols] = _sigmoid(proj(hc, 6, cols)) * p_a + _sigmoid(proj(hc, 7, cols)) * p_b
        sgc_ref[:, cols] = _sigmoid(proj(hc, 8, cols))
        gcg_ref[:, cols] = _gelu(proj(hc, 5, cols))


def _branch_call(x3, w, *, l, mod_base, tile):
    bsz, s, _ = x3.shape
    n_tiles = s // tile
    hb = tile // HALO_B
    n_hblk = s // HALO_B
    ctr = pl.BlockSpec((None, tile, D), lambda b, j: (b, j, 0))
    out = jax.ShapeDtypeStruct((bsz, s, D), F32)
    return pl.pallas_call(
        functools.partial(_branch_kernel, tile=tile, n_tiles=n_tiles),
        grid=(bsz, n_tiles),
        in_specs=[
            pl.BlockSpec((None, HALO_B, D), lambda b, j: (b, jnp.maximum(j * hb - 1, 0), 0)),
            ctr,
            pl.BlockSpec((None, HALO_B, D), lambda b, j: (b, jnp.minimum((j + 1) * hb, n_hblk - 1), 0)),
            pl.BlockSpec((None, None, N_MOD, D), lambda b, j: (l, mod_base(b), 0, 0)),
            _const_spec((None, D, N_MOD * D), (l, 0, 0)),
            _const_spec((None, 1, N_MOD * D), (l, 0, 0)),
            _layer_vec(l),
            _layer_vec(l),
            _const_spec((None, A_GROUPS, CHUNK, CHUNK), (l, 0, 0, 0)),
            _const_spec((None, CHUNK, D), (l, 0, 0)),
            _const_spec((None, CONV_W, D), (l, 0, 0)),
            _layer_vec(l),
            _layer_vec(l),
            _layer_vec(l),
            _const_spec((None, 2, D, D), (l, 0, 0, 0)),
        ],
        out_specs=[ctr, ctr, ctr],
        out_shape=[out, out, out],
        scratch_shapes=[
            pltpu.VMEM((tile + 2 * HALO_B, D), F32),
            pltpu.VMEM((tile, D), BF16),
            pltpu.VMEM((tile, D), F32),
            pltpu.VMEM((tile, D), BF16),
            pltpu.VMEM((tile, D), BF16),
        ],
        compiler_params=_params(2),
        name="mixer_branches",
    )(x3, x3, x3, w["mod"], w["win"], w["bin"], w["alg"], w["alb"], w["ws"], w["bs"],
      w["wdw"], w["bdw"], w["blg"], w["blb"], w["wbr"])


def _merge_tail(x_ref, r, part_ref, gcg_ref, sgc_ref, mod_ref, wbc_ref, wo_ref, lng_ref, lnb_ref,
                o_ref):
    chunks = [slice(c * CW, (c + 1) * CW) for c in range(D // CW)]
    y_c = (r * gcg_ref[...]).astype(BF16)
    m = jnp.concatenate(
        [(part_ref[:, cols] + sgc_ref[:, cols] * _dot(y_c, wbc_ref[:, cols])).astype(BF16)
         for cols in chunks], axis=1)
    res = jnp.concatenate(
        [ALPHA * x_ref[:, cols] + mod_ref[5:6, cols] * _dot(m, wo_ref[:, cols]) for cols in chunks],
        axis=1)
    o_ref[...] = _ln(res, lng_ref[...], lnb_ref[...])


def _merge_scan_kernel(x_ref, xcv_ref, hs_ref, part_ref, gcg_ref, sgc_ref, mod_ref,
                       wg_ref, bg_ref, lam_ref, h0f_ref, wbc_ref, wo_ref, lng_ref, lnb_ref,
                       o_ref, a_ref, b_ref, hf_ref, carry_ref, *, tile):
    _lru_coeffs(xcv_ref, wg_ref, bg_ref, lam_ref, a_ref, b_ref, 0)

    @pl.when(pl.program_id(1) == 0)
    def _():
        carry_ref[...] = h0f_ref[...]

    carry_ref[...] = _scan_rows(a_ref, b_ref, hf_ref, carry_ref[...], tile,
                                reverse=False, accumulate=False)
    _merge_tail(x_ref, hf_ref[...] + hs_ref[...], part_ref, gcg_ref, sgc_ref, mod_ref,
                wbc_ref, wo_ref, lng_ref, lnb_ref, o_ref)


def _merge_kernel(x_ref, hs_ref, part_ref, gcg_ref, sgc_ref, mod_ref,
                  wbc_ref, wo_ref, lng_ref, lnb_ref, o_ref):
    _merge_tail(x_ref, hs_ref[...], part_ref, gcg_ref, sgc_ref, mod_ref,
                wbc_ref, wo_ref, lng_ref, lnb_ref, o_ref)


def _merge_call(x3, hs, part, gcg, sgc, w, *, l, mod_base, tile, xcv=None, h0f=None):
    bsz, s, _ = x3.shape
    ctr = pl.BlockSpec((None, tile, D), lambda b, j: (b, j, 0))
    mod_spec = pl.BlockSpec((None, None, N_MOD, D), lambda b, j: (l, mod_base(b), 0, 0))
    tail_specs = [
        _const_spec((None, None, D, D), (l, 2, 0, 0)),
        _const_spec((None, D, D), (l, 0, 0)),
        _const_spec((None, None, 1, D), (l, 1, 0, 0)),
        _const_spec((None, None, 1, D), (l, 1, 0, 0)),
    ]
    tail_args = (w["wbr"], w["wo"], w["lng"], w["lnb"])
    if xcv is None:
        body = _merge_kernel
        in_specs = [ctr, ctr, ctr, ctr, ctr, mod_spec, *tail_specs]
        args = (x3, hs, part, gcg, sgc, w["mod"], *tail_args)
        scratch = []
    else:
        body = functools.partial(_merge_scan_kernel, tile=tile)
        in_specs = [ctr, ctr, ctr, ctr, ctr, ctr, mod_spec, *_lru_specs(l),
                    pl.BlockSpec((None, 1, D), lambda b, j: (b, 0, 0)), *tail_specs]
        args = (x3, xcv, hs, part, gcg, sgc, w["mod"], w["wg"], w["bg"], w["lam"], h0f, *tail_args)
        scratch = [pltpu.VMEM((tile, D), F32), pltpu.VMEM((tile, D), F32),
                   pltpu.VMEM((tile, D), F32), pltpu.VMEM((1, D), F32)]
    return pl.pallas_call(
        body,
        grid=(bsz, s // tile),
        in_specs=in_specs,
        out_specs=ctr,
        out_shape=jax.ShapeDtypeStruct((bsz, s, D), F32),
        scratch_shapes=scratch,
        compiler_params=_params(2),
        name="mixer_merge",
    )(*args)


def kernel(x, c, ctx, c_ctx, w_ada, b_ada, ln_post_g, ln_post_b, ffn_w_up, ffn_w_down,
           w_in, b_in, a_ln_g, a_ln_b, a_w_s, a_b_s, b_w_dw, b_b_dw, b_ln_g, b_ln_b,
           c_w_conv, c_b_conv, c_w_a, c_b_a, c_w_x, c_b_x, c_lam, w_br, w_o):
    bsz, seq, _ = x.shape
    ctx_len = ctx.shape[1]

    cv = jnp.concatenate(
        [c_ctx[None, :], c, jnp.zeros((MOD_ROWS - 1 - bsz, D), F32)], axis=0)
    row = lambda v: v.reshape(DEPTH, 1, -1)
    w = {
        "mod": _modulation(cv, w_ada, b_ada).reshape(DEPTH, MOD_ROWS, N_MOD, D),
        "wup": ffn_w_up.astype(BF16),
        "wdn": ffn_w_down.astype(BF16),
        "lng": ln_post_g.reshape(DEPTH, 3, 1, D),
        "lnb": ln_post_b.reshape(DEPTH, 3, 1, D),
        "win": w_in.astype(BF16),
        "bin": row(b_in),
        "alg": row(a_ln_g), "alb": row(a_ln_b),
        "ws": a_w_s.astype(BF16),
        "bs": jnp.repeat(jnp.swapaxes(a_b_s, 1, 2), A_GROUP_W, axis=2),
        "wdw": b_w_dw, "bdw": row(b_b_dw), "blg": row(b_ln_g), "blb": row(b_ln_b),
        "wcv": c_w_conv, "bcv": row(c_b_conv),
        "wg": (0.5 * jnp.concatenate([c_w_a, c_w_x], axis=-1)).astype(BF16),
        "bg": jnp.stack([c_b_a[:, 0], c_b_x[:, 0], c_b_a[:, 1], c_b_x[:, 1]], axis=1),
        "lam": c_lam,
        "wbr": w_br.astype(BF16),
        "wo": w_o.astype(BF16),
    }
    lat_mod = lambda b: 1 + b
    ctx_mod = lambda b: 0
    zeros_h = jnp.zeros((bsz, 1, D), F32)

    def ffn_lat(v, l, half):
        return _ffn(v.reshape(bsz * seq, D), w, l=l, half=half, rows_per_mod=seq,
                    mod_base=1).reshape(bsz, seq, D)

    def ffn_ctx(v, l, half):
        return _ffn(v.reshape(bsz * ctx_len, D), w, l=l, half=half, rows_per_mod=bsz * ctx_len,
                    mod_base=0).reshape(bsz, ctx_len, D)

    for l in range(DEPTH):
        ctx_out = l < DEPTH - 1
        x = ffn_lat(x, l, 0)
        ctx = ffn_ctx(ctx, l, 0)

        _, hs_ctx, end_f, end_b = _scan_call(
            ctx, w, zeros_h, zeros_h, l=l, mod_base=ctx_mod, tile=CTX_TILE, dirs=(0, 1),
            reverse=False)
        xcv_lat, hb_lat, _, _ = _scan_call(
            x, w, end_f, end_b, l=l, mod_base=lat_mod, tile=LAT_TILE, dirs=(1,), reverse=True)

        part, gcg, sgc = _branch_call(x, w, l=l, mod_base=lat_mod, tile=LAT_TILE)
        x = _merge_call(x, hb_lat, part, gcg, sgc, w, l=l, mod_base=lat_mod, tile=LAT_TILE,
                        xcv=xcv_lat, h0f=end_f)
        x = ffn_lat(x, l, 1)

        if ctx_out:
            part, gcg, sgc = _branch_call(ctx, w, l=l, mod_base=ctx_mod, tile=CTX_TILE)
            ctx = _merge_call(ctx, hs_ctx, part, gcg, sgc, w, l=l, mod_base=ctx_mod,
                              tile=CTX_TILE)
            ctx = ffn_ctx(ctx, l, 1)
    return x
```

```python
import functools

import jax
import jax.numpy as jnp
from jax import lax
from jax.experimental import pallas as pl
from jax.experimental.pallas import tpu as pltpu

D = 1024
DEPTH = 4
CHUNK = 128
A_GROUPS = 4
A_GROUP_W = D // A_GROUPS
CONV_W = 31
CONV_HALF = CONV_W // 2
C_HEADS = 8
C_HEAD_DIM = D // C_HEADS
CONV_C = 4
LRU_C = 8.0
D_FF = 2816
N_MOD = 9
ALPHA = (2.0 * DEPTH) ** 0.25
LN_EPS = 1e-5
LOG2_E = 1.4426950408889634

SUBLANES = 8
MOD_ROWS = 8
MOD_TN = 2304
FFN_TM = 1024
FFN_CK = 256
CW = A_GROUP_W
HALO_B = 16
HALO_C = 8
LAT_TILE = 512
CTX_TILE = 256
VMEM_LIMIT = 60 * 1024 * 1024

F32 = jnp.float32
BF16 = jnp.bfloat16


def _dot(a, b):
    return jnp.dot(a, b, preferred_element_type=F32)


def _ln(v, g, b):
    mu = jnp.mean(v, axis=-1, keepdims=True)
    c = v - mu
    var = jnp.mean(c * c, axis=-1, keepdims=True)
    return c * lax.rsqrt(var + LN_EPS) * g + b


def _sigmoid(v):
    return 0.5 * jnp.tanh(0.5 * v) + 0.5


def _silu(v):
    h = 0.5 * v
    return h * jnp.tanh(h) + h


def _gelu(v):
    c0 = 0.7978845608028654
    h = 0.5 * v
    return h + h * jnp.tanh(v * (c0 + (c0 * 0.044715) * (v * v)))


def _sqrt_nonneg(v):
    return v * lax.rsqrt(jnp.maximum(v, 1e-30))


def _const_spec(shape, index=None):
    index = (0,) * len(shape) if index is None else tuple(index)
    return pl.BlockSpec(shape, lambda *_: index, pipeline_mode=pl.Buffered(1))


def _layer_vec(l):
    return _const_spec((None, 1, D), (l, 0, 0))


def _params(n_grid):
    return pltpu.CompilerParams(
        dimension_semantics=("arbitrary",) * n_grid, vmem_limit_bytes=VMEM_LIMIT)


def _mod_kernel(cv_ref, w_ref, b_ref, o_ref):
    s = _silu(cv_ref[...])
    o_ref[...] = jnp.dot(s, w_ref[...], precision=lax.Precision.HIGHEST,
                         preferred_element_type=F32) + b_ref[...]


def _modulation(cv, w_ada, b_ada):
    n_cols = N_MOD * D
    return pl.pallas_call(
        _mod_kernel,
        grid=(DEPTH, n_cols // MOD_TN),
        in_specs=[
            pl.BlockSpec((MOD_ROWS, D), lambda l, n: (0, 0)),
            pl.BlockSpec((None, D, MOD_TN), lambda l, n: (l, 0, n)),
            pl.BlockSpec((None, 1, MOD_TN), lambda l, n: (l, 0, n)),
        ],
        out_specs=pl.BlockSpec((None, MOD_ROWS, MOD_TN), lambda l, n: (l, 0, n)),
        out_shape=jax.ShapeDtypeStruct((DEPTH, MOD_ROWS, n_cols), F32),
        compiler_params=_params(2),
        name="adaln_mod",
    )(cv, w_ada, b_ada.reshape(DEPTH, 1, n_cols))


def _ffn_kernel(x_ref, mod_ref, wup_ref, wdn_ref, lng_ref, lnb_ref, o_ref, act_ref, *, k):
    x = x_ref[...]
    shift = mod_ref[3 * k:3 * k + 1, :]
    scale = mod_ref[3 * k + 1:3 * k + 2, :]
    gate = mod_ref[3 * k + 2:3 * k + 3, :]
    h = (x * (1.0 + scale) + shift).astype(BF16)
    for c in range(D_FF // FFN_CK):
        lo = c * FFN_CK
        gt = _dot(h, wup_ref[:, lo:lo + FFN_CK])
        up = _dot(h, wup_ref[:, D_FF + lo:D_FF + lo + FFN_CK])
        act_ref[:, lo:lo + FFN_CK] = (_silu(gt) * up).astype(BF16)
    half_gate = 0.5 * gate
    res = jnp.concatenate(
        [ALPHA * x_ref[:, lo:lo + CW] + half_gate[:, lo:lo + CW] * _dot(act_ref[...], wdn_ref[:, lo:lo + CW])
         for lo in range(0, D, CW)], axis=1)
    o_ref[...] = _ln(res, lng_ref[...], lnb_ref[...])


def _ffn(x2, w, *, l, half, rows_per_mod, mod_base):
    n = x2.shape[0]
    k = 2 * half
    tiles_per_mod = rows_per_mod // FFN_TM
    return pl.pallas_call(
        functools.partial(_ffn_kernel, k=k),
        grid=(n // FFN_TM,),
        in_specs=[
            pl.BlockSpec((FFN_TM, D), lambda j: (j, 0)),
            pl.BlockSpec((None, None, N_MOD, D),
                         lambda j: (l, mod_base + j // tiles_per_mod, 0, 0)),
            _const_spec((None, None, D, 2 * D_FF), (l, half, 0, 0)),
            _const_spec((None, None, D_FF, D), (l, half, 0, 0)),
            _const_spec((None, None, 1, D), (l, k, 0, 0)),
            _const_spec((None, None, 1, D), (l, k, 0, 0)),
        ],
        out_specs=pl.BlockSpec((FFN_TM, D), lambda j: (j, 0)),
        out_shape=jax.ShapeDtypeStruct((n, D), F32),
        scratch_shapes=[pltpu.VMEM((FFN_TM, D_FF), BF16)],
        compiler_params=_params(1),
        name="ffn",
    )(x2, w["mod"], w["wup"], w["wdn"], w["lng"], w["lnb"])


def _lru_coeffs(xcv_ref, wg_ref, bg_ref, lam_ref, a_ref, b_ref, d):
    k = (-0.5 * LRU_C * LOG2_E) * jax.nn.softplus(-lam_ref[d:d + 1, :])
    half_br = 0.5 * bg_ref[2 * d:2 * d + 1, :]
    half_bi = 0.5 * bg_ref[2 * d + 1:2 * d + 2, :]
    for hd in range(C_HEADS):
        sl = slice(hd * C_HEAD_DIM, (hd + 1) * C_HEAD_DIM)
        xh = xcv_ref[:, sl]
        g = _dot(xh.astype(BF16), wg_ref[d, hd])
        t_r = jnp.tanh(g[:, :C_HEAD_DIM] + half_br[:, sl])
        t_i = jnp.tanh(g[:, C_HEAD_DIM:] + half_bi[:, sl])
        a = jnp.exp2(k[:, sl] * t_r + k[:, sl])
        hx = 0.5 * xh
        n_rows = a.shape[0]
        a_ref[pl.ds(hd, n_rows, stride=C_HEADS), :] = a
        b_ref[pl.ds(hd, n_rows, stride=C_HEADS), :] = _sqrt_nonneg(1.0 - a * a) * (hx * t_i + hx)


def _scan_rows(a_ref, b_ref, h_ref, h_init, n_rows, reverse):
    n_blocks = n_rows // 4
    blk_rows = 4 * C_HEADS
    order = (3, 2, 1, 0) if reverse else (0, 1, 2, 3)

    def body(s, h):
        blk = n_blocks - 1 - s if reverse else s
        rows = pl.ds(pl.multiple_of(blk * blk_rows, blk_rows), blk_rows)
        a_blk = a_ref[rows, :]
        b_blk = b_ref[rows, :]
        a = [a_blk[i * C_HEADS:(i + 1) * C_HEADS] for i in order]
        b = [b_blk[i * C_HEADS:(i + 1) * C_HEADS] for i in order]
        a01, b01 = a[1] * a[0], a[1] * b[0] + b[1]
        a23, b23 = a[3] * a[2], a[3] * b[2] + b[3]
        a03, b03 = a23 * a01, a23 * b01 + b23
        hs = [a[0] * h + b[0], a01 * h + b01, None, a03 * h + b03]
        hs[2] = a[2] * hs[1] + b[2]
        h_ref[rows, :] = jnp.concatenate([hs[order.index(i)] for i in range(4)], axis=0)
        return hs[3]
    return lax.fori_loop(0, n_blocks, body, h_init, unroll=2)


def _head_rows(h_ref, hd, n_rows):
    return h_ref[pl.ds(hd, n_rows, stride=C_HEADS), :]


def _state_spec():
    return pl.BlockSpec((None, C_HEADS, C_HEAD_DIM), lambda b, j: (b, 0, 0))


def _scan_scratch(tile):
    return [pltpu.VMEM((tile * C_HEADS, C_HEAD_DIM), F32) for _ in range(3)]


def _lru_specs(l):
    return [
        _const_spec((None, 2, C_HEADS, C_HEAD_DIM, 2 * C_HEAD_DIM), (l, 0, 0, 0, 0)),
        _const_spec((None, 4, D), (l, 0, 0)),
        _const_spec((None, 2, D), (l, 0, 0)),
    ]


def _scan_kernel(xl_ref, xc_ref, xr_ref, mod_ref, wxc_ref, bxc_ref, wcv_ref, bcv_ref,
                 wg_ref, bg_ref, lam_ref, h0f_ref, h0b_ref,
                 xcv_ref, hs_ref, endf_ref, endb_ref,
                 zx_ref, a_ref, b_ref, h_ref, carry_ref, *, tile, n_tiles, dirs, reverse):
    j = pl.program_id(1)
    jj = n_tiles - 1 - j if reverse else j
    shift = mod_ref[3:4, :]
    scale = mod_ref[4:5, :]
    xe = jnp.concatenate([xl_ref[...], xc_ref[...], xr_ref[...]], axis=0)
    he = (xe * (1.0 + scale) + shift).astype(BF16)
    zx = _dot(he, wxc_ref[...]) + bxc_ref[...]
    row = lax.broadcasted_iota(jnp.int32, (tile + 2 * HALO_C, 1), 0)
    valid = jnp.logical_and(jnp.logical_or(row >= HALO_C, jj > 0),
                            jnp.logical_or(row < tile + HALO_C, jj < n_tiles - 1))
    zx_ref[...] = jnp.where(valid, zx, 0.0)
    acc = bcv_ref[...] + wcv_ref[0:1, :] * zx_ref[pl.ds(HALO_C - 2, tile), :]
    for k in range(1, CONV_C):
        acc = acc + wcv_ref[k:k + 1, :] * zx_ref[pl.ds(HALO_C - 2 + k, tile), :]
    xcv_ref[...] = acc

    for d, end_ref in ((0, endf_ref), (1, endb_ref)):
        if d not in dirs:
            end_ref[...] = jnp.zeros_like(end_ref)
    first = True
    for d in dirs:
        _lru_coeffs(xcv_ref, wg_ref, bg_ref, lam_ref, a_ref, b_ref, d)
        h0_ref = h0b_ref if d == 1 else h0f_ref
        end_ref = endb_ref if d == 1 else endf_ref

        crows = slice(d * C_HEADS, (d + 1) * C_HEADS)

        @pl.when(j == 0)
        def _():
            carry_ref[crows, :] = h0_ref[...]

        h = _scan_rows(a_ref, b_ref, h_ref, carry_ref[crows, :], tile, reverse=(d == 1))
        carry_ref[crows, :] = h
        end_ref[...] = h
        for hd in range(C_HEADS):
            sl = slice(hd * C_HEAD_DIM, (hd + 1) * C_HEAD_DIM)
            hv = _head_rows(h_ref, hd, tile)
            hs_ref[:, sl] = hv if first else hs_ref[:, sl] + hv
        first = False


def _scan_call(x3, w, h0f, h0b, *, l, mod_base, tile, dirs, reverse):
    bsz, s, _ = x3.shape
    n_tiles = s // tile
    hb = tile // HALO_C
    n_hblk = s // HALO_C

    def ctr(b, j):
        return (b, n_tiles - 1 - j if reverse else j, 0)

    def left(b, j):
        jj = n_tiles - 1 - j if reverse else j
        return (b, jnp.maximum(jj * hb - 1, 0), 0)

    def right(b, j):
        jj = n_tiles - 1 - j if reverse else j
        return (b, jnp.minimum((jj + 1) * hb, n_hblk - 1), 0)

    vec = _state_spec()
    return pl.pallas_call(
        functools.partial(_scan_kernel, tile=tile, n_tiles=n_tiles, dirs=dirs, reverse=reverse),
        grid=(bsz, n_tiles),
        in_specs=[
            pl.BlockSpec((None, HALO_C, D), left),
            pl.BlockSpec((None, tile, D), ctr),
            pl.BlockSpec((None, HALO_C, D), right),
            pl.BlockSpec((None, None, N_MOD, D), lambda b, j: (l, mod_base(b), 0, 0)),
            _const_spec((None, D, D), (l, 0, 4)),
            _const_spec((None, 1, D), (l, 0, 4)),
            _const_spec((None, CONV_C, D), (l, 0, 0)),
            _layer_vec(l),
            *_lru_specs(l),
            vec, vec,
        ],
        out_specs=[
            pl.BlockSpec((None, tile, D), ctr),
            pl.BlockSpec((None, tile, D), ctr),
            vec, vec,
        ],
        out_shape=[
            jax.ShapeDtypeStruct((bsz, s, D), F32),
            jax.ShapeDtypeStruct((bsz, s, D), F32),
            jax.ShapeDtypeStruct((bsz, C_HEADS, C_HEAD_DIM), F32),
            jax.ShapeDtypeStruct((bsz, C_HEADS, C_HEAD_DIM), F32),
        ],
        scratch_shapes=[
            pltpu.VMEM((tile + 2 * HALO_C, D), F32),
            *_scan_scratch(tile),
            pltpu.VMEM((2 * C_HEADS, C_HEAD_DIM), F32),
        ],
        compiler_params=_params(2),
        name="lru_scan",
    )(x3, x3, x3, w["mod"], w["win"], w["bin"], w["wcv"], w["bcv"],
      w["wg"], w["bg"], w["lam"], h0f, h0b)


def _conv31(hb_ref, wdw_ref, n_rows):
    groups = {}
    for k in range(CONV_W):
        off = k + HALO_B - CONV_HALF
        groups.setdefault(off % SUBLANES, []).append((k, off - off % SUBLANES))
    acc = None
    for s, taps in sorted(groups.items()):
        n = n_rows if s == 0 else n_rows + SUBLANES
        g = None
        for k, base in taps:
            term = wdw_ref[k:k + 1, :] * hb_ref[pl.ds(base, n), :]
            g = term if g is None else g + term
        part = g if s == 0 else g[s:s + n_rows]
        acc = part if acc is None else acc + part
    return acc


def _branch_kernel(xl_ref, xc_ref, xr_ref, mod_ref, win_ref, bin_ref,
                   alg_ref, alb_ref, ws_ref, bs_ref, wdw_ref, bdw_ref, blg_ref, blb_ref, wbr_ref,
                   part_ref, gcg_ref, sgc_ref,
                   hb_ref, yb_ref, v_ref, vn_ref, ya_ref, *, tile, n_tiles):
    j = pl.program_id(1)
    shift = mod_ref[3:4, :]
    scale = mod_ref[4:5, :]
    xe = jnp.concatenate([xl_ref[...], xc_ref[...], xr_ref[...]], axis=0)
    he = (xe * (1.0 + scale) + shift).astype(BF16)
    hc = he[HALO_B:HALO_B + tile]

    def proj(lhs, group, cols):
        lo = group * D + cols.start
        return _dot(lhs, win_ref[:, lo:lo + CW]) + bin_ref[:, lo:lo + CW]

    chunks = [slice(c * CW, (c + 1) * CW) for c in range(D // CW)]

    row = lax.broadcasted_iota(jnp.int32, (tile + 2 * HALO_B, 1), 0)
    valid = jnp.logical_and(jnp.logical_or(row >= HALO_B, j > 0),
                            jnp.logical_or(row < tile + HALO_B, j < n_tiles - 1))
    for cols in chunks:
        hb_ref[:, cols] = jnp.where(valid, proj(he, 2, cols) * _sigmoid(proj(he, 3, cols)), 0.0)
    acc = _conv31(hb_ref, wdw_ref, tile) + bdw_ref[...]
    yb_ref[...] = _silu(_ln(acc, blg_ref[...], blb_ref[...])).astype(BF16)

    for cols in chunks:
        v_ref[:, cols] = proj(hc, 1, cols)
    vn_ref[...] = _ln(v_ref[...], alg_ref[...], alb_ref[...]).astype(BF16)
    for g, cols in enumerate(chunks):
        u_a = proj(hc, 0, cols)
        for n in range(tile // CHUNK):
            rows = slice(n * CHUNK, (n + 1) * CHUNK)
            mixed = _dot(ws_ref[g], vn_ref[rows, cols]) + bs_ref[:, cols]
            ya_ref[rows, cols] = (u_a[rows, :] * mixed).astype(BF16)

    for cols in chunks:
        p_a = _dot(ya_ref[...], wbr_ref[0, :, cols])
        p_b = _dot(yb_ref[...], wbr_ref[1, :, cols])
        part_ref[:, cols] = _sigmoid(proj(hc, 6, cols)) * p_a + _sigmoid(proj(hc, 7, cols)) * p_b
        sgc_ref[:, cols] = _sigmoid(proj(hc, 8, cols))
        gcg_ref[:, cols] = _gelu(proj(hc, 5, cols))


def _branch_call(x3, w, *, l, mod_base, tile):
    bsz, s, _ = x3.shape
    n_tiles = s // tile
    hb = tile // HALO_B
    n_hblk = s // HALO_B
    ctr = pl.BlockSpec((None, tile, D), lambda b, j: (b, j, 0))
    out = jax.ShapeDtypeStruct((bsz, s, D), F32)
    return pl.pallas_call(
        functools.partial(_branch_kernel, tile=tile, n_tiles=n_tiles),
        grid=(bsz, n_tiles),
        in_specs=[
            pl.BlockSpec((None, HALO_B, D), lambda b, j: (b, jnp.maximum(j * hb - 1, 0), 0)),
            ctr,
            pl.BlockSpec((None, HALO_B, D), lambda b, j: (b, jnp.minimum((j + 1) * hb, n_hblk - 1), 0)),
            pl.BlockSpec((None, None, N_MOD, D), lambda b, j: (l, mod_base(b), 0, 0)),
            _const_spec((None, D, N_MOD * D), (l, 0, 0)),
            _const_spec((None, 1, N_MOD * D), (l, 0, 0)),
            _layer_vec(l),
            _layer_vec(l),
            _const_spec((None, A_GROUPS, CHUNK, CHUNK), (l, 0, 0, 0)),
            _const_spec((None, CHUNK, D), (l, 0, 0)),
            _const_spec((None, CONV_W, D), (l, 0, 0)),
            _layer_vec(l),
            _layer_vec(l),
            _layer_vec(l),
            _const_spec((None, 2, D, D), (l, 0, 0, 0)),
        ],
        out_specs=[ctr, ctr, ctr],
        out_shape=[out, out, out],
        scratch_shapes=[
            pltpu.VMEM((tile + 2 * HALO_B, D), F32),
            pltpu.VMEM((tile, D), BF16),
            pltpu.VMEM((tile, D), F32),
            pltpu.VMEM((tile, D), BF16),
            pltpu.VMEM((tile, D), BF16),
        ],
        compiler_params=_params(2),
        name="mixer_branches",
    )(x3, x3, x3, w["mod"], w["win"], w["bin"], w["alg"], w["alb"], w["ws"], w["bs"],
      w["wdw"], w["bdw"], w["blg"], w["blb"], w["wbr"])


def _merge_tail(x_ref, r, part_ref, gcg_ref, sgc_ref, mod_ref, wbc_ref, wo_ref, lng_ref, lnb_ref,
                o_ref):
    chunks = [slice(c * CW, (c + 1) * CW) for c in range(D // CW)]
    y_c = (r * gcg_ref[...]).astype(BF16)
    m = jnp.concatenate(
        [(part_ref[:, cols] + sgc_ref[:, cols] * _dot(y_c, wbc_ref[:, cols])).astype(BF16)
         for cols in chunks], axis=1)
    res = jnp.concatenate(
        [ALPHA * x_ref[:, cols] + mod_ref[5:6, cols] * _dot(m, wo_ref[:, cols]) for cols in chunks],
        axis=1)
    o_ref[...] = _ln(res, lng_ref[...], lnb_ref[...])


def _merge_scan_kernel(x_ref, xcv_ref, hs_ref, part_ref, gcg_ref, sgc_ref, mod_ref,
                       wg_ref, bg_ref, lam_ref, h0f_ref, wbc_ref, wo_ref, lng_ref, lnb_ref,
                       o_ref, a_ref, b_ref, hf_ref, carry_ref, *, tile):
    _lru_coeffs(xcv_ref, wg_ref, bg_ref, lam_ref, a_ref, b_ref, 0)

    @pl.when(pl.program_id(1) == 0)
    def _():
        carry_ref[...] = h0f_ref[...]

    carry_ref[...] = _scan_rows(a_ref, b_ref, hf_ref, carry_ref[...], tile, reverse=False)
    r = jnp.concatenate(
        [_head_rows(hf_ref, hd, tile) + hs_ref[:, hd * C_HEAD_DIM:(hd + 1) * C_HEAD_DIM]
         for hd in range(C_HEADS)], axis=1)
    _merge_tail(x_ref, r, part_ref, gcg_ref, sgc_ref, mod_ref,
                wbc_ref, wo_ref, lng_ref, lnb_ref, o_ref)


def _merge_kernel(x_ref, hs_ref, part_ref, gcg_ref, sgc_ref, mod_ref,
                  wbc_ref, wo_ref, lng_ref, lnb_ref, o_ref):
    _merge_tail(x_ref, hs_ref[...], part_ref, gcg_ref, sgc_ref, mod_ref,
                wbc_ref, wo_ref, lng_ref, lnb_ref, o_ref)


def _merge_call(x3, hs, part, gcg, sgc, w, *, l, mod_base, tile, xcv=None, h0f=None):
    bsz, s, _ = x3.shape
    ctr = pl.BlockSpec((None, tile, D), lambda b, j: (b, j, 0))
    mod_spec = pl.BlockSpec((None, None, N_MOD, D), lambda b, j: (l, mod_base(b), 0, 0))
    tail_specs = [
        _const_spec((None, None, D, D), (l, 2, 0, 0)),
        _const_spec((None, D, D), (l, 0, 0)),
        _const_spec((None, None, 1, D), (l, 1, 0, 0)),
        _const_spec((None, None, 1, D), (l, 1, 0, 0)),
    ]
    tail_args = (w["wbr"], w["wo"], w["lng"], w["lnb"])
    if xcv is None:
        body = _merge_kernel
        in_specs = [ctr, ctr, ctr, ctr, ctr, mod_spec, *tail_specs]
        args = (x3, hs, part, gcg, sgc, w["mod"], *tail_args)
        scratch = []
    else:
        body = functools.partial(_merge_scan_kernel, tile=tile)
        in_specs = [ctr, ctr, ctr, ctr, ctr, ctr, mod_spec, *_lru_specs(l),
                    _state_spec(), *tail_specs]
        args = (x3, xcv, hs, part, gcg, sgc, w["mod"], w["wg"], w["bg"], w["lam"], h0f, *tail_args)
        scratch = [*_scan_scratch(tile), pltpu.VMEM((C_HEADS, C_HEAD_DIM), F32)]
    return pl.pallas_call(
        body,
        grid=(bsz, s // tile),
        in_specs=in_specs,
        out_specs=ctr,
        out_shape=jax.ShapeDtypeStruct((bsz, s, D), F32),
        scratch_shapes=scratch,
        compiler_params=_params(2),
        name="mixer_merge",
    )(*args)


def kernel(x, c, ctx, c_ctx, w_ada, b_ada, ln_post_g, ln_post_b, ffn_w_up, ffn_w_down,
           w_in, b_in, a_ln_g, a_ln_b, a_w_s, a_b_s, b_w_dw, b_b_dw, b_ln_g, b_ln_b,
           c_w_conv, c_b_conv, c_w_a, c_b_a, c_w_x, c_b_x, c_lam, w_br, w_o):
    bsz, seq, _ = x.shape
    ctx_len = ctx.shape[1]

    cv = jnp.concatenate(
        [c_ctx[None, :], c, jnp.zeros((MOD_ROWS - 1 - bsz, D), F32)], axis=0)
    row = lambda v: v.reshape(DEPTH, 1, -1)
    w = {
        "mod": _modulation(cv, w_ada, b_ada).reshape(DEPTH, MOD_ROWS, N_MOD, D),
        "wup": ffn_w_up.astype(BF16),
        "wdn": ffn_w_down.astype(BF16),
        "lng": ln_post_g.reshape(DEPTH, 3, 1, D),
        "lnb": ln_post_b.reshape(DEPTH, 3, 1, D),
        "win": w_in.astype(BF16),
        "bin": row(b_in),
        "alg": row(a_ln_g), "alb": row(a_ln_b),
        "ws": a_w_s.astype(BF16),
        "bs": jnp.repeat(jnp.swapaxes(a_b_s, 1, 2), A_GROUP_W, axis=2),
        "wdw": b_w_dw, "bdw": row(b_b_dw), "blg": row(b_ln_g), "blb": row(b_ln_b),
        "wcv": c_w_conv, "bcv": row(c_b_conv),
        "wg": (0.5 * jnp.concatenate([c_w_a, c_w_x], axis=-1)).astype(BF16),
        "bg": jnp.stack([c_b_a[:, 0], c_b_x[:, 0], c_b_a[:, 1], c_b_x[:, 1]], axis=1),
        "lam": c_lam,
        "wbr": w_br.astype(BF16),
        "wo": w_o.astype(BF16),
    }
    lat_mod = lambda b: 1 + b
    ctx_mod = lambda b: 0
    zeros_h = jnp.zeros((bsz, C_HEADS, C_HEAD_DIM), F32)

    def ffn_lat(v, l, half):
        return _ffn(v.reshape(bsz * seq, D), w, l=l, half=half, rows_per_mod=seq,
                    mod_base=1).reshape(bsz, seq, D)

    def ffn_ctx(v, l, half):
        return _ffn(v.reshape(bsz * ctx_len, D), w, l=l, half=half, rows_per_mod=bsz * ctx_len,
                    mod_base=0).reshape(bsz, ctx_len, D)

    for l in range(DEPTH):
        ctx_out = l < DEPTH - 1
        x = ffn_lat(x, l, 0)
        ctx = ffn_ctx(ctx, l, 0)

        _, hs_ctx, end_f, end_b = _scan_call(
            ctx, w, zeros_h, zeros_h, l=l, mod_base=ctx_mod, tile=CTX_TILE, dirs=(0, 1),
            reverse=False)
        xcv_lat, hb_lat, _, _ = _scan_call(
            x, w, end_f, end_b, l=l, mod_base=lat_mod, tile=LAT_TILE, dirs=(1,), reverse=True)

        part, gcg, sgc = _branch_call(x, w, l=l, mod_base=lat_mod, tile=LAT_TILE)
        x = _merge_call(x, hb_lat, part, gcg, sgc, w, l=l, mod_base=lat_mod, tile=LAT_TILE,
                        xcv=xcv_lat, h0f=end_f)
        x = ffn_lat(x, l, 1)

        if ctx_out:
            part, gcg, sgc = _branch_call(ctx, w, l=l, mod_base=ctx_mod, tile=CTX_TILE)
            ctx = _merge_call(ctx, hs_ctx, part, gcg, sgc, w, l=l, mod_base=ctx_mod,
                              tile=CTX_TILE)
            ctx = ffn_ctx(ctx, l, 1)
    return x
```

```python
import functools

import jax
import jax.numpy as jnp
from jax import lax
from jax.experimental import pallas as pl
from jax.experimental.pallas import tpu as pltpu

D = 1024
DEPTH = 4
CHUNK = 128
A_GROUPS = 4
A_GROUP_W = D // A_GROUPS
CONV_W = 31
CONV_HALF = CONV_W // 2
C_HEADS = 8
C_HEAD_DIM = D // C_HEADS
CONV_C = 4
LRU_C = 8.0
D_FF = 2816
N_MOD = 9
ALPHA = (2.0 * DEPTH) ** 0.25
LN_EPS = 1e-5
LOG2_E = 1.4426950408889634

SUBLANES = 8
MOD_ROWS = 8
MOD_TN = 2304
FFN_TM = 1024
FFN_CK = 256
CW = A_GROUP_W
HALO_B = 16
HALO_C = 8
LAT_TILE = 512
CTX_TILE = 256
VMEM_LIMIT = 60 * 1024 * 1024

F32 = jnp.float32
BF16 = jnp.bfloat16


def _dot(a, b):
    return jnp.dot(a, b, preferred_element_type=F32)


def _ln(v, g, b):
    mu = jnp.mean(v, axis=-1, keepdims=True)
    c = v - mu
    var = jnp.mean(c * c, axis=-1, keepdims=True)
    return c * lax.rsqrt(var + LN_EPS) * g + b


def _sigmoid(v):
    return 0.5 * jnp.tanh(0.5 * v) + 0.5


def _silu(v):
    h = 0.5 * v
    return h * jnp.tanh(h) + h


def _gelu(v):
    c0 = 0.7978845608028654
    h = 0.5 * v
    return h + h * jnp.tanh(v * (c0 + (c0 * 0.044715) * (v * v)))


def _sqrt_nonneg(v):
    return v * lax.rsqrt(jnp.maximum(v, 1e-30))


def _shift_rows(v, s, n_rows):
    n_tiles = n_rows // SUBLANES
    cur = v[0:n_rows].reshape(n_tiles, SUBLANES, v.shape[-1])
    nxt = v[SUBLANES:n_rows + SUBLANES].reshape(n_tiles, SUBLANES, v.shape[-1])
    sub = lax.broadcasted_iota(jnp.int32, cur.shape, 1)
    merged = jnp.where(sub >= s, cur, nxt)
    return pltpu.roll(merged, SUBLANES - s, 1).reshape(n_rows, v.shape[-1])


def _const_spec(shape, index=None):
    index = (0,) * len(shape) if index is None else tuple(index)
    return pl.BlockSpec(shape, lambda *_: index, pipeline_mode=pl.Buffered(1))


def _layer_vec(l):
    return _const_spec((None, 1, D), (l, 0, 0))


def _params(n_grid):
    return pltpu.CompilerParams(
        dimension_semantics=("arbitrary",) * n_grid, vmem_limit_bytes=VMEM_LIMIT)


def _mod_kernel(cv_ref, w_ref, b_ref, o_ref):
    s = _silu(cv_ref[...])
    o_ref[...] = jnp.dot(s, w_ref[...], precision=lax.Precision.HIGHEST,
                         preferred_element_type=F32) + b_ref[...]


def _modulation(cv, w_ada, b_ada):
    n_cols = N_MOD * D
    return pl.pallas_call(
        _mod_kernel,
        grid=(DEPTH, n_cols // MOD_TN),
        in_specs=[
            pl.BlockSpec((MOD_ROWS, D), lambda l, n: (0, 0)),
            pl.BlockSpec((None, D, MOD_TN), lambda l, n: (l, 0, n)),
            pl.BlockSpec((None, 1, MOD_TN), lambda l, n: (l, 0, n)),
        ],
        out_specs=pl.BlockSpec((None, MOD_ROWS, MOD_TN), lambda l, n: (l, 0, n)),
        out_shape=jax.ShapeDtypeStruct((DEPTH, MOD_ROWS, n_cols), F32),
        compiler_params=_params(2),
        name="adaln_mod",
    )(cv, w_ada, b_ada.reshape(DEPTH, 1, n_cols))


def _ffn_kernel(x_ref, mod_ref, wup_ref, wdn_ref, lng_ref, lnb_ref, o_ref, act_ref, *, k):
    x = x_ref[...]
    shift = mod_ref[3 * k:3 * k + 1, :]
    scale = mod_ref[3 * k + 1:3 * k + 2, :]
    gate = mod_ref[3 * k + 2:3 * k + 3, :]
    h = (x * (1.0 + scale) + shift).astype(BF16)
    for c in range(D_FF // FFN_CK):
        lo = c * FFN_CK
        gt = _dot(h, wup_ref[:, lo:lo + FFN_CK])
        up = _dot(h, wup_ref[:, D_FF + lo:D_FF + lo + FFN_CK])
        act_ref[:, lo:lo + FFN_CK] = (_silu(gt) * up).astype(BF16)
    half_gate = 0.5 * gate
    res = jnp.concatenate(
        [ALPHA * x_ref[:, lo:lo + CW] + half_gate[:, lo:lo + CW] * _dot(act_ref[...], wdn_ref[:, lo:lo + CW])
         for lo in range(0, D, CW)], axis=1)
    o_ref[...] = _ln(res, lng_ref[...], lnb_ref[...])


def _ffn(x2, w, *, l, half, rows_per_mod, mod_base):
    n = x2.shape[0]
    k = 2 * half
    tiles_per_mod = rows_per_mod // FFN_TM
    return pl.pallas_call(
        functools.partial(_ffn_kernel, k=k),
        grid=(n // FFN_TM,),
        in_specs=[
            pl.BlockSpec((FFN_TM, D), lambda j: (j, 0)),
            pl.BlockSpec((None, None, N_MOD, D),
                         lambda j: (l, mod_base + j // tiles_per_mod, 0, 0)),
            _const_spec((None, None, D, 2 * D_FF), (l, half, 0, 0)),
            _const_spec((None, None, D_FF, D), (l, half, 0, 0)),
            _const_spec((None, None, 1, D), (l, k, 0, 0)),
            _const_spec((None, None, 1, D), (l, k, 0, 0)),
        ],
        out_specs=pl.BlockSpec((FFN_TM, D), lambda j: (j, 0)),
        out_shape=jax.ShapeDtypeStruct((n, D), F32),
        scratch_shapes=[pltpu.VMEM((FFN_TM, D_FF), BF16)],
        compiler_params=_params(1),
        name="ffn",
    )(x2, w["mod"], w["wup"], w["wdn"], w["lng"], w["lnb"])


def _lru_coeffs(xcv_ref, wg_ref, bg_ref, lam_ref, a_ref, b_ref, d):
    k = (-0.5 * LRU_C * LOG2_E) * jax.nn.softplus(-lam_ref[d:d + 1, :])
    half_br = 0.5 * bg_ref[2 * d:2 * d + 1, :]
    half_bi = 0.5 * bg_ref[2 * d + 1:2 * d + 2, :]
    for hd in range(C_HEADS):
        sl = slice(hd * C_HEAD_DIM, (hd + 1) * C_HEAD_DIM)
        xh = xcv_ref[:, sl]
        g = _dot(xh.astype(BF16), wg_ref[d, hd])
        t_r = jnp.tanh(g[:, :C_HEAD_DIM] + half_br[:, sl])
        t_i = jnp.tanh(g[:, C_HEAD_DIM:] + half_bi[:, sl])
        a = jnp.exp2(k[:, sl] * t_r + k[:, sl])
        hx = 0.5 * xh
        n_rows = a.shape[0]
        a_ref[pl.ds(hd, n_rows, stride=C_HEADS), :] = a
        b_ref[pl.ds(hd, n_rows, stride=C_HEADS), :] = _sqrt_nonneg(1.0 - a * a) * (hx * t_i + hx)


def _scan_rows(a_ref, b_ref, h_ref, h_init, n_rows, reverse):
    n_blocks = n_rows // 4
    blk_rows = 4 * C_HEADS
    order = (3, 2, 1, 0) if reverse else (0, 1, 2, 3)

    def body(s, h):
        blk = n_blocks - 1 - s if reverse else s
        rows = pl.ds(pl.multiple_of(blk * blk_rows, blk_rows), blk_rows)
        a_blk = a_ref[rows, :]
        b_blk = b_ref[rows, :]
        a = [a_blk[i * C_HEADS:(i + 1) * C_HEADS] for i in order]
        b = [b_blk[i * C_HEADS:(i + 1) * C_HEADS] for i in order]
        a01, b01 = a[1] * a[0], a[1] * b[0] + b[1]
        a23, b23 = a[3] * a[2], a[3] * b[2] + b[3]
        a03, b03 = a23 * a01, a23 * b01 + b23
        hs = [a[0] * h + b[0], a01 * h + b01, None, a03 * h + b03]
        hs[2] = a[2] * hs[1] + b[2]
        h_ref[rows, :] = jnp.concatenate([hs[order.index(i)] for i in range(4)], axis=0)
        return hs[3]
    return lax.fori_loop(0, n_blocks, body, h_init, unroll=2)


def _head_rows(h_ref, hd, n_rows):
    return h_ref[pl.ds(hd, n_rows, stride=C_HEADS), :]


def _state_spec():
    return pl.BlockSpec((None, C_HEADS, C_HEAD_DIM), lambda b, j: (b, 0, 0))


def _scan_scratch(tile):
    return [pltpu.VMEM((tile * C_HEADS, C_HEAD_DIM), F32) for _ in range(3)]


def _lru_specs(l):
    return [
        _const_spec((None, 2, C_HEADS, C_HEAD_DIM, 2 * C_HEAD_DIM), (l, 0, 0, 0, 0)),
        _const_spec((None, 4, D), (l, 0, 0)),
        _const_spec((None, 2, D), (l, 0, 0)),
    ]


def _scan_kernel(xl_ref, xc_ref, xr_ref, mod_ref, wxc_ref, bxc_ref, wcv_ref, bcv_ref,
                 wg_ref, bg_ref, lam_ref, h0f_ref, h0b_ref,
                 xcv_ref, hs_ref, endf_ref, endb_ref,
                 zx_ref, a_ref, b_ref, h_ref, carry_ref, *, tile, n_tiles, dirs, reverse):
    j = pl.program_id(1)
    jj = n_tiles - 1 - j if reverse else j
    shift = mod_ref[3:4, :]
    scale = mod_ref[4:5, :]
    xe = jnp.concatenate([xl_ref[...], xc_ref[...], xr_ref[...]], axis=0)
    he = (xe * (1.0 + scale) + shift).astype(BF16)
    zx_ref[...] = _dot(he, wxc_ref[...]) + bxc_ref[...]
    zx_ref[0:HALO_C, :] = jnp.where(jj > 0, zx_ref[0:HALO_C, :], 0.0)
    zx_ref[tile + HALO_C:, :] = jnp.where(jj < n_tiles - 1, zx_ref[tile + HALO_C:, :], 0.0)
    acc = bcv_ref[...]
    for k in range(CONV_C):
        off = HALO_C - 2 + k
        base, s = off - off % SUBLANES, off % SUBLANES
        if s == 0:
            tap = zx_ref[pl.ds(base, tile), :]
        else:
            tap = _shift_rows(zx_ref[pl.ds(base, tile + SUBLANES), :], s, tile)
        acc = acc + wcv_ref[k:k + 1, :] * tap
    xcv_ref[...] = acc

    for d, end_ref in ((0, endf_ref), (1, endb_ref)):
        if d not in dirs:
            end_ref[...] = jnp.zeros_like(end_ref)
    first = True
    for d in dirs:
        _lru_coeffs(xcv_ref, wg_ref, bg_ref, lam_ref, a_ref, b_ref, d)
        h0_ref = h0b_ref if d == 1 else h0f_ref
        end_ref = endb_ref if d == 1 else endf_ref

        crows = slice(d * C_HEADS, (d + 1) * C_HEADS)

        @pl.when(j == 0)
        def _():
            carry_ref[crows, :] = h0_ref[...]

        h = _scan_rows(a_ref, b_ref, h_ref, carry_ref[crows, :], tile, reverse=(d == 1))
        carry_ref[crows, :] = h
        end_ref[...] = h
        for hd in range(C_HEADS):
            sl = slice(hd * C_HEAD_DIM, (hd + 1) * C_HEAD_DIM)
            hv = _head_rows(h_ref, hd, tile)
            hs_ref[:, sl] = hv if first else hs_ref[:, sl] + hv
        first = False


def _scan_call(x3, w, h0f, h0b, *, l, mod_base, tile, dirs, reverse):
    bsz, s, _ = x3.shape
    n_tiles = s // tile
    hb = tile // HALO_C
    n_hblk = s // HALO_C

    def ctr(b, j):
        return (b, n_tiles - 1 - j if reverse else j, 0)

    def left(b, j):
        jj = n_tiles - 1 - j if reverse else j
        return (b, jnp.maximum(jj * hb - 1, 0), 0)

    def right(b, j):
        jj = n_tiles - 1 - j if reverse else j
        return (b, jnp.minimum((jj + 1) * hb, n_hblk - 1), 0)

    vec = _state_spec()
    return pl.pallas_call(
        functools.partial(_scan_kernel, tile=tile, n_tiles=n_tiles, dirs=dirs, reverse=reverse),
        grid=(bsz, n_tiles),
        in_specs=[
            pl.BlockSpec((None, HALO_C, D), left),
            pl.BlockSpec((None, tile, D), ctr),
            pl.BlockSpec((None, HALO_C, D), right),
            pl.BlockSpec((None, None, N_MOD, D), lambda b, j: (l, mod_base(b), 0, 0)),
            _const_spec((None, D, D), (l, 0, 4)),
            _const_spec((None, 1, D), (l, 0, 4)),
            _const_spec((None, CONV_C, D), (l, 0, 0)),
            _layer_vec(l),
            *_lru_specs(l),
            vec, vec,
        ],
        out_specs=[
            pl.BlockSpec((None, tile, D), ctr),
            pl.BlockSpec((None, tile, D), ctr),
            vec, vec,
        ],
        out_shape=[
            jax.ShapeDtypeStruct((bsz, s, D), F32),
            jax.ShapeDtypeStruct((bsz, s, D), F32),
            jax.ShapeDtypeStruct((bsz, C_HEADS, C_HEAD_DIM), F32),
            jax.ShapeDtypeStruct((bsz, C_HEADS, C_HEAD_DIM), F32),
        ],
        scratch_shapes=[
            pltpu.VMEM((tile + 2 * HALO_C, D), F32),
            *_scan_scratch(tile),
            pltpu.VMEM((2 * C_HEADS, C_HEAD_DIM), F32),
        ],
        compiler_params=_params(2),
        name="lru_scan",
    )(x3, x3, x3, w["mod"], w["win"], w["bin"], w["wcv"], w["bcv"],
      w["wg"], w["bg"], w["lam"], h0f, h0b)


def _conv31(hb_ref, wdw_ref, n_rows):
    groups = {}
    for k in range(CONV_W):
        off = k + HALO_B - CONV_HALF
        groups.setdefault(off % SUBLANES, []).append((k, off - off % SUBLANES))
    acc = None
    for s, taps in sorted(groups.items()):
        n = n_rows if s == 0 else n_rows + SUBLANES
        g = None
        for k, base in taps:
            term = wdw_ref[k:k + 1, :] * hb_ref[pl.ds(base, n), :]
            g = term if g is None else g + term
        part = g if s == 0 else _shift_rows(g, s, n_rows)
        acc = part if acc is None else acc + part
    return acc


def _branch_kernel(xl_ref, xc_ref, xr_ref, mod_ref, win_ref, bin_ref,
                   alg_ref, alb_ref, ws_ref, bs_ref, wdw_ref, bdw_ref, blg_ref, blb_ref, wbr_ref,
                   part_ref, gcg_ref, sgc_ref,
                   hb_ref, yb_ref, v_ref, vn_ref, ya_ref, *, tile, n_tiles):
    j = pl.program_id(1)
    shift = mod_ref[3:4, :]
    scale = mod_ref[4:5, :]
    xe = jnp.concatenate([xl_ref[...], xc_ref[...], xr_ref[...]], axis=0)
    he = (xe * (1.0 + scale) + shift).astype(BF16)
    hc = he[HALO_B:HALO_B + tile]

    def proj(lhs, group, cols):
        lo = group * D + cols.start
        return _dot(lhs, win_ref[:, lo:lo + CW]) + bin_ref[:, lo:lo + CW]

    chunks = [slice(c * CW, (c + 1) * CW) for c in range(D // CW)]

    for cols in chunks:
        hb_ref[:, cols] = proj(he, 2, cols) * _sigmoid(proj(he, 3, cols))
    hb_ref[0:HALO_B, :] = jnp.where(j > 0, hb_ref[0:HALO_B, :], 0.0)
    hb_ref[tile + HALO_B:, :] = jnp.where(j < n_tiles - 1, hb_ref[tile + HALO_B:, :], 0.0)
    acc = _conv31(hb_ref, wdw_ref, tile) + bdw_ref[...]
    yb_ref[...] = _silu(_ln(acc, blg_ref[...], blb_ref[...])).astype(BF16)

    for cols in chunks:
        v_ref[:, cols] = proj(hc, 1, cols)
    vn_ref[...] = _ln(v_ref[...], alg_ref[...], alb_ref[...]).astype(BF16)
    for g, cols in enumerate(chunks):
        u_a = proj(hc, 0, cols)
        for n in range(tile // CHUNK):
            rows = slice(n * CHUNK, (n + 1) * CHUNK)
            mixed = _dot(ws_ref[g], vn_ref[rows, cols]) + bs_ref[:, cols]
            ya_ref[rows, cols] = (u_a[rows, :] * mixed).astype(BF16)

    for cols in chunks:
        p_a = _dot(ya_ref[...], wbr_ref[0, :, cols])
        p_b = _dot(yb_ref[...], wbr_ref[1, :, cols])
        part_ref[:, cols] = _sigmoid(proj(hc, 6, cols)) * p_a + _sigmoid(proj(hc, 7, cols)) * p_b
        sgc_ref[:, cols] = _sigmoid(proj(hc, 8, cols))
        gcg_ref[:, cols] = _gelu(proj(hc, 5, cols))


def _branch_call(x3, w, *, l, mod_base, tile):
    bsz, s, _ = x3.shape
    n_tiles = s // tile
    hb = tile // HALO_B
    n_hblk = s // HALO_B
    ctr = pl.BlockSpec((None, tile, D), lambda b, j: (b, j, 0))
    out = jax.ShapeDtypeStruct((bsz, s, D), F32)
    return pl.pallas_call(
        functools.partial(_branch_kernel, tile=tile, n_tiles=n_tiles),
        grid=(bsz, n_tiles),
        in_specs=[
            pl.BlockSpec((None, HALO_B, D), lambda b, j: (b, jnp.maximum(j * hb - 1, 0), 0)),
            ctr,
            pl.BlockSpec((None, HALO_B, D), lambda b, j: (b, jnp.minimum((j + 1) * hb, n_hblk - 1), 0)),
            pl.BlockSpec((None, None, N_MOD, D), lambda b, j: (l, mod_base(b), 0, 0)),
            _const_spec((None, D, N_MOD * D), (l, 0, 0)),
            _const_spec((None, 1, N_MOD * D), (l, 0, 0)),
            _layer_vec(l),
            _layer_vec(l),
            _const_spec((None, A_GROUPS, CHUNK, CHUNK), (l, 0, 0, 0)),
            _const_spec((None, CHUNK, D), (l, 0, 0)),
            _const_spec((None, CONV_W, D), (l, 0, 0)),
            _layer_vec(l),
            _layer_vec(l),
            _layer_vec(l),
            _const_spec((None, 2, D, D), (l, 0, 0, 0)),
        ],
        out_specs=[ctr, ctr, ctr],
        out_shape=[out, out, out],
        scratch_shapes=[
            pltpu.VMEM((tile + 2 * HALO_B, D), F32),
            pltpu.VMEM((tile, D), BF16),
            pltpu.VMEM((tile, D), F32),
            pltpu.VMEM((tile, D), BF16),
            pltpu.VMEM((tile, D), BF16),
        ],
        compiler_params=_params(2),
        name="mixer_branches",
    )(x3, x3, x3, w["mod"], w["win"], w["bin"], w["alg"], w["alb"], w["ws"], w["bs"],
      w["wdw"], w["bdw"], w["blg"], w["blb"], w["wbr"])


def _merge_tail(x_ref, r, part_ref, gcg_ref, sgc_ref, mod_ref, wbc_ref, wo_ref, lng_ref, lnb_ref,
                o_ref):
    chunks = [slice(c * CW, (c + 1) * CW) for c in range(D // CW)]
    y_c = (r * gcg_ref[...]).astype(BF16)
    m = jnp.concatenate(
        [(part_ref[:, cols] + sgc_ref[:, cols] * _dot(y_c, wbc_ref[:, cols])).astype(BF16)
         for cols in chunks], axis=1)
    res = jnp.concatenate(
        [ALPHA * x_ref[:, cols] + mod_ref[5:6, cols] * _dot(m, wo_ref[:, cols]) for cols in chunks],
        axis=1)
    o_ref[...] = _ln(res, lng_ref[...], lnb_ref[...])


def _merge_scan_kernel(x_ref, xcv_ref, hs_ref, part_ref, gcg_ref, sgc_ref, mod_ref,
                       wg_ref, bg_ref, lam_ref, h0f_ref, wbc_ref, wo_ref, lng_ref, lnb_ref,
                       o_ref, a_ref, b_ref, hf_ref, carry_ref, *, tile):
    _lru_coeffs(xcv_ref, wg_ref, bg_ref, lam_ref, a_ref, b_ref, 0)

    @pl.when(pl.program_id(1) == 0)
    def _():
        carry_ref[...] = h0f_ref[...]

    carry_ref[...] = _scan_rows(a_ref, b_ref, hf_ref, carry_ref[...], tile, reverse=False)
    r = jnp.concatenate(
        [_head_rows(hf_ref, hd, tile) + hs_ref[:, hd * C_HEAD_DIM:(hd + 1) * C_HEAD_DIM]
         for hd in range(C_HEADS)], axis=1)
    _merge_tail(x_ref, r, part_ref, gcg_ref, sgc_ref, mod_ref,
                wbc_ref, wo_ref, lng_ref, lnb_ref, o_ref)


def _merge_kernel(x_ref, hs_ref, part_ref, gcg_ref, sgc_ref, mod_ref,
                  wbc_ref, wo_ref, lng_ref, lnb_ref, o_ref):
    _merge_tail(x_ref, hs_ref[...], part_ref, gcg_ref, sgc_ref, mod_ref,
                wbc_ref, wo_ref, lng_ref, lnb_ref, o_ref)


def _merge_call(x3, hs, part, gcg, sgc, w, *, l, mod_base, tile, xcv=None, h0f=None):
    bsz, s, _ = x3.shape
    ctr = pl.BlockSpec((None, tile, D), lambda b, j: (b, j, 0))
    mod_spec = pl.BlockSpec((None, None, N_MOD, D), lambda b, j: (l, mod_base(b), 0, 0))
    tail_specs = [
        _const_spec((None, None, D, D), (l, 2, 0, 0)),
        _const_spec((None, D, D), (l, 0, 0)),
        _const_spec((None, None, 1, D), (l, 1, 0, 0)),
        _const_spec((None, None, 1, D), (l, 1, 0, 0)),
    ]
    tail_args = (w["wbr"], w["wo"], w["lng"], w["lnb"])
    if xcv is None:
        body = _merge_kernel
        in_specs = [ctr, ctr, ctr, ctr, ctr, mod_spec, *tail_specs]
        args = (x3, hs, part, gcg, sgc, w["mod"], *tail_args)
        scratch = []
    else:
        body = functools.partial(_merge_scan_kernel, tile=tile)
        in_specs = [ctr, ctr, ctr, ctr, ctr, ctr, mod_spec, *_lru_specs(l),
                    _state_spec(), *tail_specs]
        args = (x3, xcv, hs, part, gcg, sgc, w["mod"], w["wg"], w["bg"], w["lam"], h0f, *tail_args)
        scratch = [*_scan_scratch(tile), pltpu.VMEM((C_HEADS, C_HEAD_DIM), F32)]
    return pl.pallas_call(
        body,
        grid=(bsz, s // tile),
        in_specs=in_specs,
        out_specs=ctr,
        out_shape=jax.ShapeDtypeStruct((bsz, s, D), F32),
        scratch_shapes=scratch,
        compiler_params=_params(2),
        name="mixer_merge",
    )(*args)


def kernel(x, c, ctx, c_ctx, w_ada, b_ada, ln_post_g, ln_post_b, ffn_w_up, ffn_w_down,
           w_in, b_in, a_ln_g, a_ln_b, a_w_s, a_b_s, b_w_dw, b_b_dw, b_ln_g, b_ln_b,
           c_w_conv, c_b_conv, c_w_a, c_b_a, c_w_x, c_b_x, c_lam, w_br, w_o):
    bsz, seq, _ = x.shape
    ctx_len = ctx.shape[1]

    cv = jnp.concatenate(
        [c_ctx[None, :], c, jnp.zeros((MOD_ROWS - 1 - bsz, D), F32)], axis=0)
    row = lambda v: v.reshape(DEPTH, 1, -1)
    w = {
        "mod": _modulation(cv, w_ada, b_ada).reshape(DEPTH, MOD_ROWS, N_MOD, D),
        "wup": ffn_w_up.astype(BF16),
        "wdn": ffn_w_down.astype(BF16),
        "lng": ln_post_g.reshape(DEPTH, 3, 1, D),
        "lnb": ln_post_b.reshape(DEPTH, 3, 1, D),
        "win": w_in.astype(BF16),
        "bin": row(b_in),
        "alg": row(a_ln_g), "alb": row(a_ln_b),
        "ws": a_w_s.astype(BF16),
        "bs": jnp.repeat(jnp.swapaxes(a_b_s, 1, 2), A_GROUP_W, axis=2),
        "wdw": b_w_dw, "bdw": row(b_b_dw), "blg": row(b_ln_g), "blb": row(b_ln_b),
        "wcv": c_w_conv, "bcv": row(c_b_conv),
        "wg": (0.5 * jnp.concatenate([c_w_a, c_w_x], axis=-1)).astype(BF16),
        "bg": jnp.stack([c_b_a[:, 0], c_b_x[:, 0], c_b_a[:, 1], c_b_x[:, 1]], axis=1),
        "lam": c_lam,
        "wbr": w_br.astype(BF16),
        "wo": w_o.astype(BF16),
    }
    lat_mod = lambda b: 1 + b
    ctx_mod = lambda b: 0
    zeros_h = jnp.zeros((bsz, C_HEADS, C_HEAD_DIM), F32)

    def ffn_lat(v, l, half):
        return _ffn(v.reshape(bsz * seq, D), w, l=l, half=half, rows_per_mod=seq,
                    mod_base=1).reshape(bsz, seq, D)

    def ffn_ctx(v, l, half):
        return _ffn(v.reshape(bsz * ctx_len, D), w, l=l, half=half, rows_per_mod=bsz * ctx_len,
                    mod_base=0).reshape(bsz, ctx_len, D)

    for l in range(DEPTH):
        ctx_out = l < DEPTH - 1
        x = ffn_lat(x, l, 0)
        ctx = ffn_ctx(ctx, l, 0)

        _, hs_ctx, end_f, end_b = _scan_call(
            ctx, w, zeros_h, zeros_h, l=l, mod_base=ctx_mod, tile=CTX_TILE, dirs=(0, 1),
            reverse=False)
        xcv_lat, hb_lat, _, _ = _scan_call(
            x, w, end_f, end_b, l=l, mod_base=lat_mod, tile=LAT_TILE, dirs=(1,), reverse=True)

        part, gcg, sgc = _branch_call(x, w, l=l, mod_base=lat_mod, tile=LAT_TILE)
        x = _merge_call(x, hb_lat, part, gcg, sgc, w, l=l, mod_base=lat_mod, tile=LAT_TILE,
                        xcv=xcv_lat, h0f=end_f)
        x = ffn_lat(x, l, 1)

        if ctx_out:
            part, gcg, sgc = _branch_call(ctx, w, l=l, mod_base=ctx_mod, tile=CTX_TILE)
            ctx = _merge_call(ctx, hs_ctx, part, gcg, sgc, w, l=l, mod_base=ctx_mod,
                              tile=CTX_TILE)
            ctx = ffn_ctx(ctx, l, 1)
    return x
```

```python
import functools

import jax
import jax.numpy as jnp
from jax import lax
from jax.experimental import pallas as pl
from jax.experimental.pallas import tpu as pltpu

D = 1024
DEPTH = 4
CHUNK = 128
A_GROUPS = 4
A_GROUP_W = D // A_GROUPS
CONV_W = 31
CONV_HALF = CONV_W // 2
C_HEADS = 8
C_HEAD_DIM = D // C_HEADS
CONV_C = 4
LRU_C = 8.0
D_FF = 2816
N_MOD = 9
ALPHA = (2.0 * DEPTH) ** 0.25
LN_EPS = 1e-5
LOG2_E = 1.4426950408889634

SUBLANES = 8
MOD_ROWS = 8
MOD_TN = 2304
FFN_TM = 1024
FFN_CK = 256
CW = A_GROUP_W
HALO_B = 16
HALO_C = 8
LAT_TILE = 512
CTX_TILE = 256
VMEM_LIMIT = 60 * 1024 * 1024

F32 = jnp.float32
BF16 = jnp.bfloat16


def _dot(a, b):
    return jnp.dot(a, b, preferred_element_type=F32)


def _ln(v, g, b):
    mu = jnp.mean(v, axis=-1, keepdims=True)
    c = v - mu
    var = jnp.mean(c * c, axis=-1, keepdims=True)
    return c * lax.rsqrt(var + LN_EPS) * g + b


def _sigmoid(v):
    return 0.5 * jnp.tanh(0.5 * v) + 0.5


def _silu(v):
    h = 0.5 * v
    return h * jnp.tanh(h) + h


def _gelu(v):
    c0 = 0.7978845608028654
    h = 0.5 * v
    return h + h * jnp.tanh(v * (c0 + (c0 * 0.044715) * (v * v)))


def _sqrt_nonneg(v):
    return v * lax.rsqrt(jnp.maximum(v, 1e-30))


def _shift_rows(v, s, n_rows):
    n_tiles = n_rows // SUBLANES
    cur = v[0:n_rows].reshape(n_tiles, SUBLANES, v.shape[-1])
    nxt = v[SUBLANES:n_rows + SUBLANES].reshape(n_tiles, SUBLANES, v.shape[-1])
    sub = lax.broadcasted_iota(jnp.int32, cur.shape, 1)
    merged = jnp.where(sub >= s, cur, nxt)
    return pltpu.roll(merged, SUBLANES - s, 1).reshape(n_rows, v.shape[-1])


def _const_spec(shape, index=None):
    index = (0,) * len(shape) if index is None else tuple(index)
    return pl.BlockSpec(shape, lambda *_: index, pipeline_mode=pl.Buffered(1))


def _layer_vec(l):
    return _const_spec((None, 1, D), (l, 0, 0))


def _params(n_grid):
    return pltpu.CompilerParams(
        dimension_semantics=("arbitrary",) * n_grid, vmem_limit_bytes=VMEM_LIMIT)


def _mod_kernel(cv_ref, w_ref, b_ref, o_ref):
    s = _silu(cv_ref[...])
    o_ref[...] = jnp.dot(s, w_ref[...], precision=lax.Precision.HIGHEST,
                         preferred_element_type=F32) + b_ref[...]


def _modulation(cv, w_ada, b_ada):
    n_cols = N_MOD * D
    return pl.pallas_call(
        _mod_kernel,
        grid=(DEPTH, n_cols // MOD_TN),
        in_specs=[
            pl.BlockSpec((MOD_ROWS, D), lambda l, n: (0, 0)),
            pl.BlockSpec((None, D, MOD_TN), lambda l, n: (l, 0, n)),
            pl.BlockSpec((None, 1, MOD_TN), lambda l, n: (l, 0, n)),
        ],
        out_specs=pl.BlockSpec((None, MOD_ROWS, MOD_TN), lambda l, n: (l, 0, n)),
        out_shape=jax.ShapeDtypeStruct((DEPTH, MOD_ROWS, n_cols), F32),
        compiler_params=_params(2),
        name="adaln_mod",
    )(cv, w_ada, b_ada.reshape(DEPTH, 1, n_cols))


def _ffn_kernel(x_ref, mod_ref, wup_ref, wdn_ref, lng_ref, lnb_ref, o_ref, act_ref, *, k):
    x = x_ref[...]
    shift = mod_ref[3 * k:3 * k + 1, :]
    scale = mod_ref[3 * k + 1:3 * k + 2, :]
    gate = mod_ref[3 * k + 2:3 * k + 3, :]
    h = (x * (1.0 + scale) + shift).astype(BF16)
    for c in range(D_FF // FFN_CK):
        lo = c * FFN_CK
        gt = _dot(h, wup_ref[:, lo:lo + FFN_CK])
        up = _dot(h, wup_ref[:, D_FF + lo:D_FF + lo + FFN_CK])
        act_ref[:, lo:lo + FFN_CK] = (_silu(gt) * up).astype(BF16)
    half_gate = 0.5 * gate
    res = jnp.concatenate(
        [ALPHA * x_ref[:, lo:lo + CW] + half_gate[:, lo:lo + CW] * _dot(act_ref[...], wdn_ref[:, lo:lo + CW])
         for lo in range(0, D, CW)], axis=1)
    o_ref[...] = _ln(res, lng_ref[...], lnb_ref[...])


def _ffn(x2, w, *, l, half, rows_per_mod, mod_base):
    n = x2.shape[0]
    k = 2 * half
    tiles_per_mod = rows_per_mod // FFN_TM
    return pl.pallas_call(
        functools.partial(_ffn_kernel, k=k),
        grid=(n // FFN_TM,),
        in_specs=[
            pl.BlockSpec((FFN_TM, D), lambda j: (j, 0)),
            pl.BlockSpec((None, None, N_MOD, D),
                         lambda j: (l, mod_base + j // tiles_per_mod, 0, 0)),
            _const_spec((None, None, D, 2 * D_FF), (l, half, 0, 0)),
            _const_spec((None, None, D_FF, D), (l, half, 0, 0)),
            _const_spec((None, None, 1, D), (l, k, 0, 0)),
            _const_spec((None, None, 1, D), (l, k, 0, 0)),
        ],
        out_specs=pl.BlockSpec((FFN_TM, D), lambda j: (j, 0)),
        out_shape=jax.ShapeDtypeStruct((n, D), F32),
        scratch_shapes=[pltpu.VMEM((FFN_TM, D_FF), BF16)],
        compiler_params=_params(1),
        name="ffn",
    )(x2, w["mod"], w["wup"], w["wdn"], w["lng"], w["lnb"])


def _lru_coeffs(xcv_ref, wg_ref, bg_ref, lam_ref, a_ref, b_ref, d):
    k = (-0.5 * LRU_C * LOG2_E) * jax.nn.softplus(-lam_ref[d:d + 1, :])
    half_br = 0.5 * bg_ref[2 * d:2 * d + 1, :]
    half_bi = 0.5 * bg_ref[2 * d + 1:2 * d + 2, :]
    for hd in range(C_HEADS):
        sl = slice(hd * C_HEAD_DIM, (hd + 1) * C_HEAD_DIM)
        xh = xcv_ref[:, sl]
        g = _dot(xh.astype(BF16), wg_ref[d, hd])
        t_r = jnp.tanh(g[:, :C_HEAD_DIM] + half_br[:, sl])
        t_i = jnp.tanh(g[:, C_HEAD_DIM:] + half_bi[:, sl])
        a = jnp.exp2(k[:, sl] * t_r + k[:, sl])
        hx = 0.5 * xh
        n_rows = a.shape[0]
        a_ref[pl.ds(hd, n_rows, stride=C_HEADS), :] = a
        b_ref[pl.ds(hd, n_rows, stride=C_HEADS), :] = _sqrt_nonneg(1.0 - a * a) * (hx * t_i + hx)


def _scan_rows(a_ref, b_ref, h_ref, h_init, n_rows, reverse, add_ref=None):
    n_blocks = n_rows // 4
    blk_rows = 4 * C_HEADS
    order = (3, 2, 1, 0) if reverse else (0, 1, 2, 3)

    def body(s, h):
        blk = n_blocks - 1 - s if reverse else s
        rows = pl.ds(pl.multiple_of(blk * blk_rows, blk_rows), blk_rows)
        a_blk = a_ref[rows, :]
        b_blk = b_ref[rows, :]
        a = [a_blk[i * C_HEADS:(i + 1) * C_HEADS] for i in order]
        b = [b_blk[i * C_HEADS:(i + 1) * C_HEADS] for i in order]
        a01, b01 = a[1] * a[0], a[1] * b[0] + b[1]
        a23, b23 = a[3] * a[2], a[3] * b[2] + b[3]
        a03, b03 = a23 * a01, a23 * b01 + b23
        hs = [a[0] * h + b[0], a01 * h + b01, None, a03 * h + b03]
        hs[2] = a[2] * hs[1] + b[2]
        h_blk = jnp.concatenate([hs[order.index(i)] for i in range(4)], axis=0)
        h_ref[rows, :] = h_blk if add_ref is None else h_blk + add_ref[rows, :]
        return hs[3]
    return lax.fori_loop(0, n_blocks, body, h_init, unroll=2)


def _head_rows(h_ref, hd, n_rows):
    return h_ref[pl.ds(hd, n_rows, stride=C_HEADS), :]


def _time_major(h_ref, n_rows):
    return jnp.concatenate([_head_rows(h_ref, hd, n_rows) for hd in range(C_HEADS)], axis=1)


def _state_spec():
    return pl.BlockSpec((None, C_HEADS, C_HEAD_DIM), lambda b, j: (b, 0, 0))


def _scan_scratch(tile, n):
    return [pltpu.VMEM((tile * C_HEADS, C_HEAD_DIM), F32) for _ in range(n)]


def _lru_specs(l):
    return [
        _const_spec((None, 2, C_HEADS, C_HEAD_DIM, 2 * C_HEAD_DIM), (l, 0, 0, 0, 0)),
        _const_spec((None, 4, D), (l, 0, 0)),
        _const_spec((None, 2, D), (l, 0, 0)),
    ]


def _scan_kernel(xl_ref, xc_ref, xr_ref, mod_ref, wxc_ref, bxc_ref, wcv_ref, bcv_ref,
                 wg_ref, bg_ref, lam_ref, h0f_ref, h0b_ref,
                 xcv_ref, hs_ref, endf_ref, endb_ref,
                 zx_ref, a_ref, b_ref, carry_ref, *, tile, n_tiles, dirs, reverse):
    j = pl.program_id(1)
    jj = n_tiles - 1 - j if reverse else j
    shift = mod_ref[3:4, :]
    scale = mod_ref[4:5, :]
    xe = jnp.concatenate([xl_ref[...], xc_ref[...], xr_ref[...]], axis=0)
    he = (xe * (1.0 + scale) + shift).astype(BF16)
    zx_ref[...] = _dot(he, wxc_ref[...]) + bxc_ref[...]
    zx_ref[0:HALO_C, :] = jnp.where(jj > 0, zx_ref[0:HALO_C, :], 0.0)
    zx_ref[tile + HALO_C:, :] = jnp.where(jj < n_tiles - 1, zx_ref[tile + HALO_C:, :], 0.0)
    acc = bcv_ref[...]
    for k in range(CONV_C):
        off = HALO_C - 2 + k
        base, s = off - off % SUBLANES, off % SUBLANES
        if s == 0:
            tap = zx_ref[pl.ds(base, tile), :]
        else:
            tap = _shift_rows(zx_ref[pl.ds(base, tile + SUBLANES), :], s, tile)
        acc = acc + wcv_ref[k:k + 1, :] * tap
    xcv_ref[...] = acc

    for d, end_ref in ((0, endf_ref), (1, endb_ref)):
        if d not in dirs:
            end_ref[...] = jnp.zeros_like(end_ref)
    first = True
    for d in dirs:
        _lru_coeffs(xcv_ref, wg_ref, bg_ref, lam_ref, a_ref, b_ref, d)
        h0_ref = h0b_ref if d == 1 else h0f_ref
        end_ref = endb_ref if d == 1 else endf_ref

        crows = slice(d * C_HEADS, (d + 1) * C_HEADS)

        @pl.when(j == 0)
        def _():
            carry_ref[crows, :] = h0_ref[...]

        h = _scan_rows(a_ref, b_ref, hs_ref, carry_ref[crows, :], tile, reverse=(d == 1),
                       add_ref=None if first else hs_ref)
        carry_ref[crows, :] = h
        end_ref[...] = h
        first = False


def _scan_call(x3, w, h0f, h0b, *, l, mod_base, tile, dirs, reverse):
    bsz, s, _ = x3.shape
    n_tiles = s // tile
    hb = tile // HALO_C
    n_hblk = s // HALO_C

    def ctr(b, j):
        return (b, n_tiles - 1 - j if reverse else j, 0)

    def left(b, j):
        jj = n_tiles - 1 - j if reverse else j
        return (b, jnp.maximum(jj * hb - 1, 0), 0)

    def right(b, j):
        jj = n_tiles - 1 - j if reverse else j
        return (b, jnp.minimum((jj + 1) * hb, n_hblk - 1), 0)

    vec = _state_spec()
    return pl.pallas_call(
        functools.partial(_scan_kernel, tile=tile, n_tiles=n_tiles, dirs=dirs, reverse=reverse),
        grid=(bsz, n_tiles),
        in_specs=[
            pl.BlockSpec((None, HALO_C, D), left),
            pl.BlockSpec((None, tile, D), ctr),
            pl.BlockSpec((None, HALO_C, D), right),
            pl.BlockSpec((None, None, N_MOD, D), lambda b, j: (l, mod_base(b), 0, 0)),
            _const_spec((None, D, D), (l, 0, 4)),
            _const_spec((None, 1, D), (l, 0, 4)),
            _const_spec((None, CONV_C, D), (l, 0, 0)),
            _layer_vec(l),
            *_lru_specs(l),
            vec, vec,
        ],
        out_specs=[
            pl.BlockSpec((None, tile, D), ctr),
            pl.BlockSpec((None, tile * C_HEADS, C_HEAD_DIM), ctr),
            vec, vec,
        ],
        out_shape=[
            jax.ShapeDtypeStruct((bsz, s, D), F32),
            jax.ShapeDtypeStruct((bsz, s * C_HEADS, C_HEAD_DIM), F32),
            jax.ShapeDtypeStruct((bsz, C_HEADS, C_HEAD_DIM), F32),
            jax.ShapeDtypeStruct((bsz, C_HEADS, C_HEAD_DIM), F32),
        ],
        scratch_shapes=[
            pltpu.VMEM((tile + 2 * HALO_C, D), F32),
            *_scan_scratch(tile, 2),
            pltpu.VMEM((2 * C_HEADS, C_HEAD_DIM), F32),
        ],
        compiler_params=_params(2),
        name="lru_scan",
    )(x3, x3, x3, w["mod"], w["win"], w["bin"], w["wcv"], w["bcv"],
      w["wg"], w["bg"], w["lam"], h0f, h0b)


def _conv31(hb_ref, wdw_ref, n_rows):
    groups = {}
    for k in range(CONV_W):
        off = k + HALO_B - CONV_HALF
        groups.setdefault(off % SUBLANES, []).append((k, off - off % SUBLANES))
    acc = None
    for s, taps in sorted(groups.items()):
        n = n_rows if s == 0 else n_rows + SUBLANES
        g = None
        for k, base in taps:
            term = wdw_ref[k:k + 1, :] * hb_ref[pl.ds(base, n), :]
            g = term if g is None else g + term
        part = g if s == 0 else _shift_rows(g, s, n_rows)
        acc = part if acc is None else acc + part
    return acc


def _branch_kernel(xl_ref, xc_ref, xr_ref, mod_ref, win_ref, bin_ref,
                   alg_ref, alb_ref, ws_ref, bs_ref, wdw_ref, bdw_ref, blg_ref, blb_ref, wbr_ref,
                   part_ref, gcg_ref, sgc_ref,
                   hb_ref, yb_ref, v_ref, vn_ref, ya_ref, *, tile, n_tiles):
    j = pl.program_id(1)
    shift = mod_ref[3:4, :]
    scale = mod_ref[4:5, :]
    xe = jnp.concatenate([xl_ref[...], xc_ref[...], xr_ref[...]], axis=0)
    he = (xe * (1.0 + scale) + shift).astype(BF16)
    hc = he[HALO_B:HALO_B + tile]

    def proj(lhs, group, cols):
        lo = group * D + cols.start
        return _dot(lhs, win_ref[:, lo:lo + CW]) + bin_ref[:, lo:lo + CW]

    chunks = [slice(c * CW, (c + 1) * CW) for c in range(D // CW)]

    for cols in chunks:
        hb_ref[:, cols] = proj(he, 2, cols) * _sigmoid(proj(he, 3, cols))
    hb_ref[0:HALO_B, :] = jnp.where(j > 0, hb_ref[0:HALO_B, :], 0.0)
    hb_ref[tile + HALO_B:, :] = jnp.where(j < n_tiles - 1, hb_ref[tile + HALO_B:, :], 0.0)
    acc = _conv31(hb_ref, wdw_ref, tile) + bdw_ref[...]
    yb_ref[...] = _silu(_ln(acc, blg_ref[...], blb_ref[...])).astype(BF16)

    for cols in chunks:
        v_ref[:, cols] = proj(hc, 1, cols)
    vn_ref[...] = _ln(v_ref[...], alg_ref[...], alb_ref[...]).astype(BF16)
    for g, cols in enumerate(chunks):
        u_a = proj(hc, 0, cols)
        for n in range(tile // CHUNK):
            rows = slice(n * CHUNK, (n + 1) * CHUNK)
            mixed = _dot(ws_ref[g], vn_ref[rows, cols]) + bs_ref[:, cols]
            ya_ref[rows, cols] = (u_a[rows, :] * mixed).astype(BF16)

    for cols in chunks:
        p_a = _dot(ya_ref[...], wbr_ref[0, :, cols])
        p_b = _dot(yb_ref[...], wbr_ref[1, :, cols])
        part_ref[:, cols] = _sigmoid(proj(hc, 6, cols)) * p_a + _sigmoid(proj(hc, 7, cols)) * p_b
        sgc_ref[:, cols] = _sigmoid(proj(hc, 8, cols))
        gcg_ref[:, cols] = _gelu(proj(hc, 5, cols))


def _branch_call(x3, w, *, l, mod_base, tile):
    bsz, s, _ = x3.shape
    n_tiles = s // tile
    hb = tile // HALO_B
    n_hblk = s // HALO_B
    ctr = pl.BlockSpec((None, tile, D), lambda b, j: (b, j, 0))
    out = jax.ShapeDtypeStruct((bsz, s, D), F32)
    return pl.pallas_call(
        functools.partial(_branch_kernel, tile=tile, n_tiles=n_tiles),
        grid=(bsz, n_tiles),
        in_specs=[
            pl.BlockSpec((None, HALO_B, D), lambda b, j: (b, jnp.maximum(j * hb - 1, 0), 0)),
            ctr,
            pl.BlockSpec((None, HALO_B, D), lambda b, j: (b, jnp.minimum((j + 1) * hb, n_hblk - 1), 0)),
            pl.BlockSpec((None, None, N_MOD, D), lambda b, j: (l, mod_base(b), 0, 0)),
            _const_spec((None, D, N_MOD * D), (l, 0, 0)),
            _const_spec((None, 1, N_MOD * D), (l, 0, 0)),
            _layer_vec(l),
            _layer_vec(l),
            _const_spec((None, A_GROUPS, CHUNK, CHUNK), (l, 0, 0, 0)),
            _const_spec((None, CHUNK, D), (l, 0, 0)),
            _const_spec((None, CONV_W, D), (l, 0, 0)),
            _layer_vec(l),
            _layer_vec(l),
            _layer_vec(l),
            _const_spec((None, 2, D, D), (l, 0, 0, 0)),
        ],
        out_specs=[ctr, ctr, ctr],
        out_shape=[out, out, out],
        scratch_shapes=[
            pltpu.VMEM((tile + 2 * HALO_B, D), F32),
            pltpu.VMEM((tile, D), BF16),
            pltpu.VMEM((tile, D), F32),
            pltpu.VMEM((tile, D), BF16),
            pltpu.VMEM((tile, D), BF16),
        ],
        compiler_params=_params(2),
        name="mixer_branches",
    )(x3, x3, x3, w["mod"], w["win"], w["bin"], w["alg"], w["alb"], w["ws"], w["bs"],
      w["wdw"], w["bdw"], w["blg"], w["blb"], w["wbr"])


def _merge_tail(x_ref, r, part_ref, gcg_ref, sgc_ref, mod_ref, wbc_ref, wo_ref, lng_ref, lnb_ref,
                o_ref):
    chunks = [slice(c * CW, (c + 1) * CW) for c in range(D // CW)]
    y_c = (r * gcg_ref[...]).astype(BF16)
    m = jnp.concatenate(
        [(part_ref[:, cols] + sgc_ref[:, cols] * _dot(y_c, wbc_ref[:, cols])).astype(BF16)
         for cols in chunks], axis=1)
    res = jnp.concatenate(
        [ALPHA * x_ref[:, cols] + mod_ref[5:6, cols] * _dot(m, wo_ref[:, cols]) for cols in chunks],
        axis=1)
    o_ref[...] = _ln(res, lng_ref[...], lnb_ref[...])


def _merge_scan_kernel(x_ref, xcv_ref, hs_ref, part_ref, gcg_ref, sgc_ref, mod_ref,
                       wg_ref, bg_ref, lam_ref, h0f_ref, wbc_ref, wo_ref, lng_ref, lnb_ref,
                       o_ref, a_ref, b_ref, hf_ref, carry_ref, *, tile):
    _lru_coeffs(xcv_ref, wg_ref, bg_ref, lam_ref, a_ref, b_ref, 0)

    @pl.when(pl.program_id(1) == 0)
    def _():
        carry_ref[...] = h0f_ref[...]

    carry_ref[...] = _scan_rows(a_ref, b_ref, hf_ref, carry_ref[...], tile, reverse=False,
                                add_ref=hs_ref)
    _merge_tail(x_ref, _time_major(hf_ref, tile), part_ref, gcg_ref, sgc_ref, mod_ref,
                wbc_ref, wo_ref, lng_ref, lnb_ref, o_ref)


def _merge_kernel(x_ref, hs_ref, part_ref, gcg_ref, sgc_ref, mod_ref,
                  wbc_ref, wo_ref, lng_ref, lnb_ref, o_ref, *, tile):
    _merge_tail(x_ref, _time_major(hs_ref, tile), part_ref, gcg_ref, sgc_ref, mod_ref,
                wbc_ref, wo_ref, lng_ref, lnb_ref, o_ref)


def _merge_call(x3, hs, part, gcg, sgc, w, *, l, mod_base, tile, xcv=None, h0f=None):
    bsz, s, _ = x3.shape
    ctr = pl.BlockSpec((None, tile, D), lambda b, j: (b, j, 0))
    mod_spec = pl.BlockSpec((None, None, N_MOD, D), lambda b, j: (l, mod_base(b), 0, 0))
    tail_specs = [
        _const_spec((None, None, D, D), (l, 2, 0, 0)),
        _const_spec((None, D, D), (l, 0, 0)),
        _const_spec((None, None, 1, D), (l, 1, 0, 0)),
        _const_spec((None, None, 1, D), (l, 1, 0, 0)),
    ]
    tail_args = (w["wbr"], w["wo"], w["lng"], w["lnb"])
    hs_spec = pl.BlockSpec((None, tile * C_HEADS, C_HEAD_DIM), lambda b, j: (b, j, 0))
    if xcv is None:
        body = functools.partial(_merge_kernel, tile=tile)
        in_specs = [ctr, hs_spec, ctr, ctr, ctr, mod_spec, *tail_specs]
        args = (x3, hs, part, gcg, sgc, w["mod"], *tail_args)
        scratch = []
    else:
        body = functools.partial(_merge_scan_kernel, tile=tile)
        in_specs = [ctr, ctr, hs_spec, ctr, ctr, ctr, mod_spec, *_lru_specs(l),
                    _state_spec(), *tail_specs]
        args = (x3, xcv, hs, part, gcg, sgc, w["mod"], w["wg"], w["bg"], w["lam"], h0f, *tail_args)
        scratch = [*_scan_scratch(tile, 3), pltpu.VMEM((C_HEADS, C_HEAD_DIM), F32)]
    return pl.pallas_call(
        body,
        grid=(bsz, s // tile),
        in_specs=in_specs,
        out_specs=ctr,
        out_shape=jax.ShapeDtypeStruct((bsz, s, D), F32),
        scratch_shapes=scratch,
        compiler_params=_params(2),
        name="mixer_merge",
    )(*args)


def kernel(x, c, ctx, c_ctx, w_ada, b_ada, ln_post_g, ln_post_b, ffn_w_up, ffn_w_down,
           w_in, b_in, a_ln_g, a_ln_b, a_w_s, a_b_s, b_w_dw, b_b_dw, b_ln_g, b_ln_b,
           c_w_conv, c_b_conv, c_w_a, c_b_a, c_w_x, c_b_x, c_lam, w_br, w_o):
    bsz, seq, _ = x.shape
    ctx_len = ctx.shape[1]

    cv = jnp.concatenate(
        [c_ctx[None, :], c, jnp.zeros((MOD_ROWS - 1 - bsz, D), F32)], axis=0)
    row = lambda v: v.reshape(DEPTH, 1, -1)
    w = {
        "mod": _modulation(cv, w_ada, b_ada).reshape(DEPTH, MOD_ROWS, N_MOD, D),
        "wup": ffn_w_up.astype(BF16),
        "wdn": ffn_w_down.astype(BF16),
        "lng": ln_post_g.reshape(DEPTH, 3, 1, D),
        "lnb": ln_post_b.reshape(DEPTH, 3, 1, D),
        "win": w_in.astype(BF16),
        "bin": row(b_in),
        "alg": row(a_ln_g), "alb": row(a_ln_b),
        "ws": a_w_s.astype(BF16),
        "bs": jnp.repeat(jnp.swapaxes(a_b_s, 1, 2), A_GROUP_W, axis=2),
        "wdw": b_w_dw, "bdw": row(b_b_dw), "blg": row(b_ln_g), "blb": row(b_ln_b),
        "wcv": c_w_conv, "bcv": row(c_b_conv),
        "wg": (0.5 * jnp.concatenate([c_w_a, c_w_x], axis=-1)).astype(BF16),
        "bg": jnp.stack([c_b_a[:, 0], c_b_x[:, 0], c_b_a[:, 1], c_b_x[:, 1]], axis=1),
        "lam": c_lam,
        "wbr": w_br.astype(BF16),
        "wo": w_o.astype(BF16),
    }
    lat_mod = lambda b: 1 + b
    ctx_mod = lambda b: 0
    zeros_h = jnp.zeros((bsz, C_HEADS, C_HEAD_DIM), F32)

    def ffn_lat(v, l, half):
        return _ffn(v.reshape(bsz * seq, D), w, l=l, half=half, rows_per_mod=seq,
                    mod_base=1).reshape(bsz, seq, D)

    def ffn_ctx(v, l, half):
        return _ffn(v.reshape(bsz * ctx_len, D), w, l=l, half=half, rows_per_mod=bsz * ctx_len,
                    mod_base=0).reshape(bsz, ctx_len, D)

    for l in range(DEPTH):
        ctx_out = l < DEPTH - 1
        x = ffn_lat(x, l, 0)
        ctx = ffn_ctx(ctx, l, 0)

        _, hs_ctx, end_f, end_b = _scan_call(
            ctx, w, zeros_h, zeros_h, l=l, mod_base=ctx_mod, tile=CTX_TILE, dirs=(0, 1),
            reverse=False)
        xcv_lat, hb_lat, _, _ = _scan_call(
            x, w, end_f, end_b, l=l, mod_base=lat_mod, tile=LAT_TILE, dirs=(1,), reverse=True)

        part, gcg, sgc = _branch_call(x, w, l=l, mod_base=lat_mod, tile=LAT_TILE)
        x = _merge_call(x, hb_lat, part, gcg, sgc, w, l=l, mod_base=lat_mod, tile=LAT_TILE,
                        xcv=xcv_lat, h0f=end_f)
        x = ffn_lat(x, l, 1)

        if ctx_out:
            part, gcg, sgc = _branch_call(ctx, w, l=l, mod_base=ctx_mod, tile=CTX_TILE)
            ctx = _merge_call(ctx, hs_ctx, part, gcg, sgc, w, l=l, mod_base=ctx_mod,
                              tile=CTX_TILE)
            ctx = ffn_ctx(ctx, l, 1)
    return x
```

```python
import functools

import jax
import jax.numpy as jnp
from jax import lax
from jax.experimental import pallas as pl
from jax.experimental.pallas import tpu as pltpu

D = 1024
DEPTH = 4
CHUNK = 128
A_GROUPS = 4
A_GROUP_W = D // A_GROUPS
CONV_W = 31
CONV_HALF = CONV_W // 2
C_HEADS = 8
C_HEAD_DIM = D // C_HEADS
CONV_C = 4
LRU_C = 8.0
D_FF = 2816
N_MOD = 9
ALPHA = (2.0 * DEPTH) ** 0.25
LN_EPS = 1e-5
LOG2_E = 1.4426950408889634

SUBLANES = 8
MOD_ROWS = 8
MOD_TN = 2304
FFN_TM = 1024
FFN_CK = 256
CW = A_GROUP_W
HALO_B = 16
HALO_C = 8
LAT_TILE = 512
CTX_TILE = 256
VMEM_LIMIT = 60 * 1024 * 1024

F32 = jnp.float32
BF16 = jnp.bfloat16


def _dot(a, b):
    return jnp.dot(a, b, preferred_element_type=F32)


def _ln(v, g, b):
    mu = jnp.mean(v, axis=-1, keepdims=True)
    c = v - mu
    var = jnp.mean(c * c, axis=-1, keepdims=True)
    return c * lax.rsqrt(var + LN_EPS) * g + b


def _sigmoid(v):
    return 0.5 * jnp.tanh(0.5 * v) + 0.5


def _silu(v):
    h = 0.5 * v
    return h * jnp.tanh(h) + h


def _gelu(v):
    c0 = 0.7978845608028654
    h = 0.5 * v
    return h + h * jnp.tanh(v * (c0 + (c0 * 0.044715) * (v * v)))


def _sqrt_nonneg(v):
    return v * lax.rsqrt(jnp.maximum(v, 1e-30))


def _shift_rows(v, s, n_rows):
    n_tiles = n_rows // SUBLANES
    cur = v[0:n_rows].reshape(n_tiles, SUBLANES, v.shape[-1])
    nxt = v[SUBLANES:n_rows + SUBLANES].reshape(n_tiles, SUBLANES, v.shape[-1])
    sub = lax.broadcasted_iota(jnp.int32, cur.shape, 1)
    merged = jnp.where(sub >= s, cur, nxt)
    return pltpu.roll(merged, SUBLANES - s, 1).reshape(n_rows, v.shape[-1])


def _const_spec(shape, index=None):
    index = (0,) * len(shape) if index is None else tuple(index)
    return pl.BlockSpec(shape, lambda *_: index, pipeline_mode=pl.Buffered(1))


def _layer_vec(l):
    return _const_spec((None, 1, D), (l, 0, 0))


def _params(n_grid):
    return pltpu.CompilerParams(
        dimension_semantics=("arbitrary",) * n_grid, vmem_limit_bytes=VMEM_LIMIT)


def _mod_kernel(cv_ref, w_ref, b_ref, o_ref):
    s = _silu(cv_ref[...])
    o_ref[...] = jnp.dot(s, w_ref[...], precision=lax.Precision.HIGHEST,
                         preferred_element_type=F32) + b_ref[...]


def _modulation(cv, w_ada, b_ada):
    n_cols = N_MOD * D
    return pl.pallas_call(
        _mod_kernel,
        grid=(DEPTH, n_cols // MOD_TN),
        in_specs=[
            pl.BlockSpec((MOD_ROWS, D), lambda l, n: (0, 0)),
            pl.BlockSpec((None, D, MOD_TN), lambda l, n: (l, 0, n)),
            pl.BlockSpec((None, 1, MOD_TN), lambda l, n: (l, 0, n)),
        ],
        out_specs=pl.BlockSpec((None, MOD_ROWS, MOD_TN), lambda l, n: (l, 0, n)),
        out_shape=jax.ShapeDtypeStruct((DEPTH, MOD_ROWS, n_cols), F32),
        compiler_params=_params(2),
        name="adaln_mod",
    )(cv, w_ada, b_ada.reshape(DEPTH, 1, n_cols))


def _ffn_kernel(x_ref, mod_ref, wup_ref, wdn_ref, lng_ref, lnb_ref, o_ref, act_ref, *, k):
    x = x_ref[...]
    shift = mod_ref[3 * k:3 * k + 1, :]
    scale = mod_ref[3 * k + 1:3 * k + 2, :]
    gate = mod_ref[3 * k + 2:3 * k + 3, :]
    h = (x * (1.0 + scale) + shift).astype(BF16)
    for c in range(D_FF // FFN_CK):
        lo = c * FFN_CK
        gt = _dot(h, wup_ref[:, lo:lo + FFN_CK])
        up = _dot(h, wup_ref[:, D_FF + lo:D_FF + lo + FFN_CK])
        act_ref[:, lo:lo + FFN_CK] = (_silu(gt) * up).astype(BF16)
    half_gate = 0.5 * gate
    res = jnp.concatenate(
        [ALPHA * x_ref[:, lo:lo + CW] + half_gate[:, lo:lo + CW] * _dot(act_ref[...], wdn_ref[:, lo:lo + CW])
         for lo in range(0, D, CW)], axis=1)
    o_ref[...] = _ln(res, lng_ref[...], lnb_ref[...])


def _ffn(x2, w, *, l, half, rows_per_mod, mod_base):
    n = x2.shape[0]
    k = 2 * half
    tiles_per_mod = rows_per_mod // FFN_TM
    return pl.pallas_call(
        functools.partial(_ffn_kernel, k=k),
        grid=(n // FFN_TM,),
        in_specs=[
            pl.BlockSpec((FFN_TM, D), lambda j: (j, 0)),
            pl.BlockSpec((None, None, N_MOD, D),
                         lambda j: (l, mod_base + j // tiles_per_mod, 0, 0)),
            _const_spec((None, None, D, 2 * D_FF), (l, half, 0, 0)),
            _const_spec((None, None, D_FF, D), (l, half, 0, 0)),
            _const_spec((None, None, 1, D), (l, k, 0, 0)),
            _const_spec((None, None, 1, D), (l, k, 0, 0)),
        ],
        out_specs=pl.BlockSpec((FFN_TM, D), lambda j: (j, 0)),
        out_shape=jax.ShapeDtypeStruct((n, D), F32),
        scratch_shapes=[pltpu.VMEM((FFN_TM, D_FF), BF16)],
        compiler_params=_params(1),
        name="ffn",
    )(x2, w["mod"], w["wup"], w["wdn"], w["lng"], w["lnb"])


def _lru_coeffs(xcv_ref, wg_ref, bg_ref, lam_ref, a_ref, b_ref, d):
    k = (-0.5 * LRU_C * LOG2_E) * jax.nn.softplus(-lam_ref[d:d + 1, :])
    half_br = 0.5 * bg_ref[2 * d:2 * d + 1, :]
    half_bi = 0.5 * bg_ref[2 * d + 1:2 * d + 2, :]
    for hd in range(C_HEADS):
        sl = slice(hd * C_HEAD_DIM, (hd + 1) * C_HEAD_DIM)
        xh = xcv_ref[:, sl]
        g = _dot(xh.astype(BF16), wg_ref[d, hd])
        t_r = jnp.tanh(g[:, :C_HEAD_DIM] + half_br[:, sl])
        t_i = jnp.tanh(g[:, C_HEAD_DIM:] + half_bi[:, sl])
        a = jnp.exp2(k[:, sl] * t_r + k[:, sl])
        hx = 0.5 * xh
        n_rows = a.shape[0]
        a_ref[pl.ds(hd, n_rows, stride=C_HEADS), :] = a
        b_ref[pl.ds(hd, n_rows, stride=C_HEADS), :] = _sqrt_nonneg(1.0 - a * a) * (hx * t_i + hx)


def _scan_rows(a_ref, b_ref, h_ref, h_init, n_rows, reverse, add_ref=None):
    n_blocks = n_rows // 4
    blk_rows = 4 * C_HEADS
    order = (3, 2, 1, 0) if reverse else (0, 1, 2, 3)

    def body(s, h):
        blk = n_blocks - 1 - s if reverse else s
        rows = pl.ds(pl.multiple_of(blk * blk_rows, blk_rows), blk_rows)
        a_blk = a_ref[rows, :]
        b_blk = b_ref[rows, :]
        a = [a_blk[i * C_HEADS:(i + 1) * C_HEADS] for i in order]
        b = [b_blk[i * C_HEADS:(i + 1) * C_HEADS] for i in order]
        a01, b01 = a[1] * a[0], a[1] * b[0] + b[1]
        a23, b23 = a[3] * a[2], a[3] * b[2] + b[3]
        a03, b03 = a23 * a01, a23 * b01 + b23
        hs = [a[0] * h + b[0], a01 * h + b01, None, a03 * h + b03]
        hs[2] = a[2] * hs[1] + b[2]
        h_blk = jnp.concatenate([hs[order.index(i)] for i in range(4)], axis=0)
        h_ref[rows, :] = h_blk if add_ref is None else h_blk + add_ref[rows, :]
        return hs[3]
    return lax.fori_loop(0, n_blocks, body, h_init, unroll=2)


def _head_rows(h_ref, hd, n_rows):
    return h_ref[pl.ds(hd, n_rows, stride=C_HEADS), :]


def _time_major(h_ref, n_rows):
    return jnp.concatenate([_head_rows(h_ref, hd, n_rows) for hd in range(C_HEADS)], axis=1)


def _state_spec():
    return pl.BlockSpec((None, C_HEADS, C_HEAD_DIM), lambda b, j: (b, 0, 0))


def _scan_scratch(tile, n):
    return [pltpu.VMEM((tile * C_HEADS, C_HEAD_DIM), F32) for _ in range(n)]


def _lru_specs(l):
    return [
        _const_spec((None, 2, C_HEADS, C_HEAD_DIM, 2 * C_HEAD_DIM), (l, 0, 0, 0, 0)),
        _const_spec((None, 4, D), (l, 0, 0)),
        _const_spec((None, 2, D), (l, 0, 0)),
    ]


def _scan_kernel(xl_ref, xc_ref, xr_ref, mod_ref, wxc_ref, bxc_ref, wcv_ref, bcv_ref,
                 wg_ref, bg_ref, lam_ref, h0f_ref, h0b_ref,
                 xcv_ref, hs_ref, endf_ref, endb_ref,
                 zx_ref, a_ref, b_ref, carry_ref, *, tile, n_tiles, dirs, reverse):
    j = pl.program_id(1)
    jj = n_tiles - 1 - j if reverse else j
    shift = mod_ref[3:4, :]
    scale = mod_ref[4:5, :]
    xe = jnp.concatenate([xl_ref[...], xc_ref[...], xr_ref[...]], axis=0)
    he = (xe * (1.0 + scale) + shift).astype(BF16)
    zx_ref[...] = _dot(he, wxc_ref[...]) + bxc_ref[...]
    zx_ref[0:HALO_C, :] = jnp.where(jj > 0, zx_ref[0:HALO_C, :], 0.0)
    zx_ref[tile + HALO_C:, :] = jnp.where(jj < n_tiles - 1, zx_ref[tile + HALO_C:, :], 0.0)
    acc = bcv_ref[...]
    for k in range(CONV_C):
        off = HALO_C - 2 + k
        base, s = off - off % SUBLANES, off % SUBLANES
        if s == 0:
            tap = zx_ref[pl.ds(base, tile), :]
        else:
            tap = _shift_rows(zx_ref[pl.ds(base, tile + SUBLANES), :], s, tile)
        acc = acc + wcv_ref[k:k + 1, :] * tap
    xcv_ref[...] = acc

    for d, end_ref in ((0, endf_ref), (1, endb_ref)):
        if d not in dirs:
            end_ref[...] = jnp.zeros_like(end_ref)
    first = True
    for d in dirs:
        _lru_coeffs(xcv_ref, wg_ref, bg_ref, lam_ref, a_ref, b_ref, d)
        h0_ref = h0b_ref if d == 1 else h0f_ref
        end_ref = endb_ref if d == 1 else endf_ref

        crows = slice(d * C_HEADS, (d + 1) * C_HEADS)

        @pl.when(j == 0)
        def _():
            carry_ref[crows, :] = h0_ref[...]

        h = _scan_rows(a_ref, b_ref, hs_ref, carry_ref[crows, :], tile, reverse=(d == 1),
                       add_ref=None if first else hs_ref)
        carry_ref[crows, :] = h
        end_ref[...] = h
        first = False


def _scan_call(x3, w, h0f, h0b, *, l, mod_base, tile, dirs, reverse):
    bsz, s, _ = x3.shape
    n_tiles = s // tile
    hb = tile // HALO_C
    n_hblk = s // HALO_C

    def ctr(b, j):
        return (b, n_tiles - 1 - j if reverse else j, 0)

    def left(b, j):
        jj = n_tiles - 1 - j if reverse else j
        return (b, jnp.maximum(jj * hb - 1, 0), 0)

    def right(b, j):
        jj = n_tiles - 1 - j if reverse else j
        return (b, jnp.minimum((jj + 1) * hb, n_hblk - 1), 0)

    vec = _state_spec()
    return pl.pallas_call(
        functools.partial(_scan_kernel, tile=tile, n_tiles=n_tiles, dirs=dirs, reverse=reverse),
        grid=(bsz, n_tiles),
        in_specs=[
            pl.BlockSpec((None, HALO_C, D), left),
            pl.BlockSpec((None, tile, D), ctr),
            pl.BlockSpec((None, HALO_C, D), right),
            pl.BlockSpec((None, None, N_MOD, D), lambda b, j: (l, mod_base(b), 0, 0)),
            _const_spec((None, D, D), (l, 0, 4)),
            _const_spec((None, 1, D), (l, 0, 4)),
            _const_spec((None, CONV_C, D), (l, 0, 0)),
            _layer_vec(l),
            *_lru_specs(l),
            vec, vec,
        ],
        out_specs=[
            pl.BlockSpec((None, tile, D), ctr),
            pl.BlockSpec((None, tile * C_HEADS, C_HEAD_DIM), ctr),
            vec, vec,
        ],
        out_shape=[
            jax.ShapeDtypeStruct((bsz, s, D), F32),
            jax.ShapeDtypeStruct((bsz, s * C_HEADS, C_HEAD_DIM), F32),
            jax.ShapeDtypeStruct((bsz, C_HEADS, C_HEAD_DIM), F32),
            jax.ShapeDtypeStruct((bsz, C_HEADS, C_HEAD_DIM), F32),
        ],
        scratch_shapes=[
            pltpu.VMEM((tile + 2 * HALO_C, D), F32),
            *_scan_scratch(tile, 2),
            pltpu.VMEM((2 * C_HEADS, C_HEAD_DIM), F32),
        ],
        compiler_params=_params(2),
        name="lru_scan",
    )(x3, x3, x3, w["mod"], w["win"], w["bin"], w["wcv"], w["bcv"],
      w["wg"], w["bg"], w["lam"], h0f, h0b)


def _conv31(hb_ref, wdw_ref, n_rows):
    groups = {}
    for k in range(CONV_W):
        off = k + HALO_B - CONV_HALF
        groups.setdefault(off % SUBLANES, []).append((k, off - off % SUBLANES))
    acc = None
    for s, taps in sorted(groups.items()):
        n = n_rows if s == 0 else n_rows + SUBLANES
        g = None
        for k, base in taps:
            term = wdw_ref[k:k + 1, :] * hb_ref[pl.ds(base, n), :]
            g = term if g is None else g + term
        part = g if s == 0 else _shift_rows(g, s, n_rows)
        acc = part if acc is None else acc + part
    return acc


def _branch_kernel(xl_ref, xc_ref, xr_ref, mod_ref, win_ref, bin_ref,
                   alg_ref, alb_ref, ws_ref, bs_ref, wdw_ref, bdw_ref, blg_ref, blb_ref, wbr_ref,
                   part_ref, gcg_ref, sgc_ref,
                   hb_ref, yb_ref, v_ref, vn_ref, ya_ref, *, tile, n_tiles):
    j = pl.program_id(1)
    shift = mod_ref[3:4, :]
    scale = mod_ref[4:5, :]
    xe = jnp.concatenate([xl_ref[...], xc_ref[...], xr_ref[...]], axis=0)
    he = (xe * (1.0 + scale) + shift).astype(BF16)
    hc = he[HALO_B:HALO_B + tile]

    def proj(lhs, group, cols):
        lo = group * D + cols.start
        return _dot(lhs, win_ref[:, lo:lo + CW]) + bin_ref[:, lo:lo + CW]

    chunks = [slice(c * CW, (c + 1) * CW) for c in range(D // CW)]

    for cols in chunks:
        hb_ref[:, cols] = proj(he, 2, cols) * _sigmoid(proj(he, 3, cols))
    hb_ref[0:HALO_B, :] = jnp.where(j > 0, hb_ref[0:HALO_B, :], 0.0)
    hb_ref[tile + HALO_B:, :] = jnp.where(j < n_tiles - 1, hb_ref[tile + HALO_B:, :], 0.0)
    acc = _conv31(hb_ref, wdw_ref, tile) + bdw_ref[...]
    yb_ref[...] = _silu(_ln(acc, blg_ref[...], blb_ref[...])).astype(BF16)

    for cols in chunks:
        v_ref[:, cols] = proj(hc, 1, cols)
    vn_ref[...] = _ln(v_ref[...], alg_ref[...], alb_ref[...]).astype(BF16)
    for g, cols in enumerate(chunks):
        u_a = proj(hc, 0, cols)
        for n in range(tile // CHUNK):
            rows = slice(n * CHUNK, (n + 1) * CHUNK)
            mixed = _dot(ws_ref[g], vn_ref[rows, cols]) + bs_ref[:, cols]
            ya_ref[rows, cols] = (u_a[rows, :] * mixed).astype(BF16)

    for cols in chunks:
        p_a = _dot(ya_ref[...], wbr_ref[0, :, cols])
        p_b = _dot(yb_ref[...], wbr_ref[1, :, cols])
        part_ref[:, cols] = (_sigmoid(proj(hc, 6, cols)) * p_a
                             + _sigmoid(proj(hc, 7, cols)) * p_b).astype(BF16)
        sgc_ref[:, cols] = _sigmoid(proj(hc, 8, cols)).astype(BF16)
        gcg_ref[:, cols] = _gelu(proj(hc, 5, cols)).astype(BF16)


def _branch_call(x3, w, *, l, mod_base, tile):
    bsz, s, _ = x3.shape
    n_tiles = s // tile
    hb = tile // HALO_B
    n_hblk = s // HALO_B
    ctr = pl.BlockSpec((None, tile, D), lambda b, j: (b, j, 0))
    out = jax.ShapeDtypeStruct((bsz, s, D), BF16)
    return pl.pallas_call(
        functools.partial(_branch_kernel, tile=tile, n_tiles=n_tiles),
        grid=(bsz, n_tiles),
        in_specs=[
            pl.BlockSpec((None, HALO_B, D), lambda b, j: (b, jnp.maximum(j * hb - 1, 0), 0)),
            ctr,
            pl.BlockSpec((None, HALO_B, D), lambda b, j: (b, jnp.minimum((j + 1) * hb, n_hblk - 1), 0)),
            pl.BlockSpec((None, None, N_MOD, D), lambda b, j: (l, mod_base(b), 0, 0)),
            _const_spec((None, D, N_MOD * D), (l, 0, 0)),
            _const_spec((None, 1, N_MOD * D), (l, 0, 0)),
            _layer_vec(l),
            _layer_vec(l),
            _const_spec((None, A_GROUPS, CHUNK, CHUNK), (l, 0, 0, 0)),
            _const_spec((None, CHUNK, D), (l, 0, 0)),
            _const_spec((None, CONV_W, D), (l, 0, 0)),
            _layer_vec(l),
            _layer_vec(l),
            _layer_vec(l),
            _const_spec((None, 2, D, D), (l, 0, 0, 0)),
        ],
        out_specs=[ctr, ctr, ctr],
        out_shape=[out, out, out],
        scratch_shapes=[
            pltpu.VMEM((tile + 2 * HALO_B, D), F32),
            pltpu.VMEM((tile, D), BF16),
            pltpu.VMEM((tile, D), F32),
            pltpu.VMEM((tile, D), BF16),
            pltpu.VMEM((tile, D), BF16),
        ],
        compiler_params=_params(2),
        name="mixer_branches",
    )(x3, x3, x3, w["mod"], w["win"], w["bin"], w["alg"], w["alb"], w["ws"], w["bs"],
      w["wdw"], w["bdw"], w["blg"], w["blb"], w["wbr"])


def _merge_tail(x_ref, r, part_ref, gcg_ref, sgc_ref, mod_ref, wbc_ref, wo_ref, lng_ref, lnb_ref,
                o_ref):
    chunks = [slice(c * CW, (c + 1) * CW) for c in range(D // CW)]
    y_c = (r * gcg_ref[...]).astype(BF16)
    m = jnp.concatenate(
        [(part_ref[:, cols] + sgc_ref[:, cols] * _dot(y_c, wbc_ref[:, cols])).astype(BF16)
         for cols in chunks], axis=1)
    res = jnp.concatenate(
        [ALPHA * x_ref[:, cols] + mod_ref[5:6, cols] * _dot(m, wo_ref[:, cols]) for cols in chunks],
        axis=1)
    o_ref[...] = _ln(res, lng_ref[...], lnb_ref[...])


def _merge_scan_kernel(x_ref, xcv_ref, hs_ref, part_ref, gcg_ref, sgc_ref, mod_ref,
                       wg_ref, bg_ref, lam_ref, h0f_ref, wbc_ref, wo_ref, lng_ref, lnb_ref,
                       o_ref, a_ref, b_ref, hf_ref, carry_ref, *, tile):
    _lru_coeffs(xcv_ref, wg_ref, bg_ref, lam_ref, a_ref, b_ref, 0)

    @pl.when(pl.program_id(1) == 0)
    def _():
        carry_ref[...] = h0f_ref[...]

    carry_ref[...] = _scan_rows(a_ref, b_ref, hf_ref, carry_ref[...], tile, reverse=False,
                                add_ref=hs_ref)
    _merge_tail(x_ref, _time_major(hf_ref, tile), part_ref, gcg_ref, sgc_ref, mod_ref,
                wbc_ref, wo_ref, lng_ref, lnb_ref, o_ref)


def _merge_kernel(x_ref, hs_ref, part_ref, gcg_ref, sgc_ref, mod_ref,
                  wbc_ref, wo_ref, lng_ref, lnb_ref, o_ref, *, tile):
    _merge_tail(x_ref, _time_major(hs_ref, tile), part_ref, gcg_ref, sgc_ref, mod_ref,
                wbc_ref, wo_ref, lng_ref, lnb_ref, o_ref)


def _merge_call(x3, hs, part, gcg, sgc, w, *, l, mod_base, tile, xcv=None, h0f=None):
    bsz, s, _ = x3.shape
    ctr = pl.BlockSpec((None, tile, D), lambda b, j: (b, j, 0))
    mod_spec = pl.BlockSpec((None, None, N_MOD, D), lambda b, j: (l, mod_base(b), 0, 0))
    tail_specs = [
        _const_spec((None, None, D, D), (l, 2, 0, 0)),
        _const_spec((None, D, D), (l, 0, 0)),
        _const_spec((None, None, 1, D), (l, 1, 0, 0)),
        _const_spec((None, None, 1, D), (l, 1, 0, 0)),
    ]
    tail_args = (w["wbr"], w["wo"], w["lng"], w["lnb"])
    hs_spec = pl.BlockSpec((None, tile * C_HEADS, C_HEAD_DIM), lambda b, j: (b, j, 0))
    if xcv is None:
        body = functools.partial(_merge_kernel, tile=tile)
        in_specs = [ctr, hs_spec, ctr, ctr, ctr, mod_spec, *tail_specs]
        args = (x3, hs, part, gcg, sgc, w["mod"], *tail_args)
        scratch = []
    else:
        body = functools.partial(_merge_scan_kernel, tile=tile)
        in_specs = [ctr, ctr, hs_spec, ctr, ctr, ctr, mod_spec, *_lru_specs(l),
                    _state_spec(), *tail_specs]
        args = (x3, xcv, hs, part, gcg, sgc, w["mod"], w["wg"], w["bg"], w["lam"], h0f, *tail_args)
        scratch = [*_scan_scratch(tile, 3), pltpu.VMEM((C_HEADS, C_HEAD_DIM), F32)]
    return pl.pallas_call(
        body,
        grid=(bsz, s // tile),
        in_specs=in_specs,
        out_specs=ctr,
        out_shape=jax.ShapeDtypeStruct((bsz, s, D), F32),
        scratch_shapes=scratch,
        compiler_params=_params(2),
        name="mixer_merge",
    )(*args)


def kernel(x, c, ctx, c_ctx, w_ada, b_ada, ln_post_g, ln_post_b, ffn_w_up, ffn_w_down,
           w_in, b_in, a_ln_g, a_ln_b, a_w_s, a_b_s, b_w_dw, b_b_dw, b_ln_g, b_ln_b,
           c_w_conv, c_b_conv, c_w_a, c_b_a, c_w_x, c_b_x, c_lam, w_br, w_o):
    bsz, seq, _ = x.shape
    ctx_len = ctx.shape[1]

    cv = jnp.concatenate(
        [c_ctx[None, :], c, jnp.zeros((MOD_ROWS - 1 - bsz, D), F32)], axis=0)
    row = lambda v: v.reshape(DEPTH, 1, -1)
    w = {
        "mod": _modulation(cv, w_ada, b_ada).reshape(DEPTH, MOD_ROWS, N_MOD, D),
        "wup": ffn_w_up.astype(BF16),
        "wdn": ffn_w_down.astype(BF16),
        "lng": ln_post_g.reshape(DEPTH, 3, 1, D),
        "lnb": ln_post_b.reshape(DEPTH, 3, 1, D),
        "win": w_in.astype(BF16),
        "bin": row(b_in),
        "alg": row(a_ln_g), "alb": row(a_ln_b),
        "ws": a_w_s.astype(BF16),
        "bs": jnp.repeat(jnp.swapaxes(a_b_s, 1, 2), A_GROUP_W, axis=2),
        "wdw": b_w_dw, "bdw": row(b_b_dw), "blg": row(b_ln_g), "blb": row(b_ln_b),
        "wcv": c_w_conv, "bcv": row(c_b_conv),
        "wg": (0.5 * jnp.concatenate([c_w_a, c_w_x], axis=-1)).astype(BF16),
        "bg": jnp.stack([c_b_a[:, 0], c_b_x[:, 0], c_b_a[:, 1], c_b_x[:, 1]], axis=1),
        "lam": c_lam,
        "wbr": w_br.astype(BF16),
        "wo": w_o.astype(BF16),
    }
    lat_mod = lambda b: 1 + b
    ctx_mod = lambda b: 0
    zeros_h = jnp.zeros((bsz, C_HEADS, C_HEAD_DIM), F32)

    def ffn_lat(v, l, half):
        return _ffn(v.reshape(bsz * seq, D), w, l=l, half=half, rows_per_mod=seq,
                    mod_base=1).reshape(bsz, seq, D)

    def ffn_ctx(v, l, half):
        return _ffn(v.reshape(bsz * ctx_len, D), w, l=l, half=half, rows_per_mod=bsz * ctx_len,
                    mod_base=0).reshape(bsz, ctx_len, D)

    for l in range(DEPTH):
        ctx_out = l < DEPTH - 1
        x = ffn_lat(x, l, 0)
        ctx = ffn_ctx(ctx, l, 0)

        _, hs_ctx, end_f, end_b = _scan_call(
            ctx, w, zeros_h, zeros_h, l=l, mod_base=ctx_mod, tile=CTX_TILE, dirs=(0, 1),
            reverse=False)
        xcv_lat, hb_lat, _, _ = _scan_call(
            x, w, end_f, end_b, l=l, mod_base=lat_mod, tile=LAT_TILE, dirs=(1,), reverse=True)

        part, gcg, sgc = _branch_call(x, w, l=l, mod_base=lat_mod, tile=LAT_TILE)
        x = _merge_call(x, hb_lat, part, gcg, sgc, w, l=l, mod_base=lat_mod, tile=LAT_TILE,
                        xcv=xcv_lat, h0f=end_f)
        x = ffn_lat(x, l, 1)

        if ctx_out:
            part, gcg, sgc = _branch_call(ctx, w, l=l, mod_base=ctx_mod, tile=CTX_TILE)
            ctx = _merge_call(ctx, hs_ctx, part, gcg, sgc, w, l=l, mod_base=ctx_mod,
                              tile=CTX_TILE)
            ctx = ffn_ctx(ctx, l, 1)
    return x
```

```python
import functools

import jax
import jax.numpy as jnp
from jax import lax
from jax.experimental import pallas as pl
from jax.experimental.pallas import tpu as pltpu

D = 1024
DEPTH = 4
CHUNK = 128
A_GROUPS = 4
A_GROUP_W = D // A_GROUPS
CONV_W = 31
CONV_HALF = CONV_W // 2
C_HEADS = 8
C_HEAD_DIM = D // C_HEADS
CONV_C = 4
LRU_C = 8.0
D_FF = 2816
N_MOD = 9
ALPHA = (2.0 * DEPTH) ** 0.25
LN_EPS = 1e-5
LOG2_E = 1.4426950408889634

SUBLANES = 8
MOD_ROWS = 8
MOD_TN = 2304
FFN_TM = 1024
FFN_CK = 256
CW = A_GROUP_W
HALO_B = 16
HALO_C = 8
LAT_TILE = 512
SCAN_TILE = 1024
CTX_TILE = 256
VMEM_LIMIT = 60 * 1024 * 1024

F32 = jnp.float32
BF16 = jnp.bfloat16


def _dot(a, b):
    return jnp.dot(a, b, preferred_element_type=F32)


def _ln(v, g, b):
    mu = jnp.mean(v, axis=-1, keepdims=True)
    c = v - mu
    var = jnp.mean(c * c, axis=-1, keepdims=True)
    return c * lax.rsqrt(var + LN_EPS) * g + b


def _sigmoid(v):
    return 0.5 * jnp.tanh(0.5 * v) + 0.5


def _silu(v):
    h = 0.5 * v
    return h * jnp.tanh(h) + h


def _gelu(v):
    c0 = 0.7978845608028654
    h = 0.5 * v
    return h + h * jnp.tanh(v * (c0 + (c0 * 0.044715) * (v * v)))


def _sqrt_nonneg(v):
    return v * lax.rsqrt(jnp.maximum(v, 1e-30))


def _shift_rows(v, s, n_rows):
    n_tiles = n_rows // SUBLANES
    cur = v[0:n_rows].reshape(n_tiles, SUBLANES, v.shape[-1])
    nxt = v[SUBLANES:n_rows + SUBLANES].reshape(n_tiles, SUBLANES, v.shape[-1])
    sub = lax.broadcasted_iota(jnp.int32, cur.shape, 1)
    merged = jnp.where(sub >= s, cur, nxt)
    return pltpu.roll(merged, SUBLANES - s, 1).reshape(n_rows, v.shape[-1])


def _const_spec(shape, index=None):
    index = (0,) * len(shape) if index is None else tuple(index)
    return pl.BlockSpec(shape, lambda *_: index, pipeline_mode=pl.Buffered(1))


def _layer_vec(l):
    return _const_spec((None, 1, D), (l, 0, 0))


def _params(n_grid):
    return pltpu.CompilerParams(
        dimension_semantics=("arbitrary",) * n_grid, vmem_limit_bytes=VMEM_LIMIT)


def _mod_kernel(cv_ref, w_ref, b_ref, o_ref):
    s = _silu(cv_ref[...])
    o_ref[...] = jnp.dot(s, w_ref[...], precision=lax.Precision.HIGHEST,
                         preferred_element_type=F32) + b_ref[...]


def _modulation(cv, w_ada, b_ada):
    n_cols = N_MOD * D
    return pl.pallas_call(
        _mod_kernel,
        grid=(DEPTH, n_cols // MOD_TN),
        in_specs=[
            pl.BlockSpec((MOD_ROWS, D), lambda l, n: (0, 0)),
            pl.BlockSpec((None, D, MOD_TN), lambda l, n: (l, 0, n)),
            pl.BlockSpec((None, 1, MOD_TN), lambda l, n: (l, 0, n)),
        ],
        out_specs=pl.BlockSpec((None, MOD_ROWS, MOD_TN), lambda l, n: (l, 0, n)),
        out_shape=jax.ShapeDtypeStruct((DEPTH, MOD_ROWS, n_cols), F32),
        compiler_params=_params(2),
        name="adaln_mod",
    )(cv, w_ada, b_ada.reshape(DEPTH, 1, n_cols))


def _ffn_kernel(x_ref, mod_ref, wup_ref, wdn_ref, lng_ref, lnb_ref, o_ref, act_ref, *, k):
    x = x_ref[...]
    shift = mod_ref[3 * k:3 * k + 1, :]
    scale = mod_ref[3 * k + 1:3 * k + 2, :]
    gate = mod_ref[3 * k + 2:3 * k + 3, :]
    h = (x * (1.0 + scale) + shift).astype(BF16)
    for c in range(D_FF // FFN_CK):
        lo = c * FFN_CK
        gt = _dot(h, wup_ref[:, lo:lo + FFN_CK])
        up = _dot(h, wup_ref[:, D_FF + lo:D_FF + lo + FFN_CK])
        act_ref[:, lo:lo + FFN_CK] = (_silu(gt) * up).astype(BF16)
    half_gate = 0.5 * gate
    res = jnp.concatenate(
        [ALPHA * x_ref[:, lo:lo + CW] + half_gate[:, lo:lo + CW] * _dot(act_ref[...], wdn_ref[:, lo:lo + CW])
         for lo in range(0, D, CW)], axis=1)
    o_ref[...] = _ln(res, lng_ref[...], lnb_ref[...])


def _ffn(x2, w, *, l, half, rows_per_mod, mod_base):
    n = x2.shape[0]
    k = 2 * half
    tiles_per_mod = rows_per_mod // FFN_TM
    return pl.pallas_call(
        functools.partial(_ffn_kernel, k=k),
        grid=(n // FFN_TM,),
        in_specs=[
            pl.BlockSpec((FFN_TM, D), lambda j: (j, 0)),
            pl.BlockSpec((None, None, N_MOD, D),
                         lambda j: (l, mod_base + j // tiles_per_mod, 0, 0)),
            _const_spec((None, None, D, 2 * D_FF), (l, half, 0, 0)),
            _const_spec((None, None, D_FF, D), (l, half, 0, 0)),
            _const_spec((None, None, 1, D), (l, k, 0, 0)),
            _const_spec((None, None, 1, D), (l, k, 0, 0)),
        ],
        out_specs=pl.BlockSpec((FFN_TM, D), lambda j: (j, 0)),
        out_shape=jax.ShapeDtypeStruct((n, D), F32),
        scratch_shapes=[pltpu.VMEM((FFN_TM, D_FF), BF16)],
        compiler_params=_params(1),
        name="ffn",
    )(x2, w["mod"], w["wup"], w["wdn"], w["lng"], w["lnb"])


def _lru_coeffs(xcv_ref, wg_ref, bg_ref, lam_ref, a_ref, b_ref, d):
    k = (-0.5 * LRU_C * LOG2_E) * jax.nn.softplus(-lam_ref[d:d + 1, :])
    half_br = 0.5 * bg_ref[2 * d:2 * d + 1, :]
    half_bi = 0.5 * bg_ref[2 * d + 1:2 * d + 2, :]
    for hd in range(C_HEADS):
        sl = slice(hd * C_HEAD_DIM, (hd + 1) * C_HEAD_DIM)
        xh = xcv_ref[:, sl]
        g = _dot(xh.astype(BF16), wg_ref[d, hd])
        t_r = jnp.tanh(g[:, :C_HEAD_DIM] + half_br[:, sl])
        t_i = jnp.tanh(g[:, C_HEAD_DIM:] + half_bi[:, sl])
        a = jnp.exp2(k[:, sl] * t_r + k[:, sl])
        hx = 0.5 * xh
        n_rows = a.shape[0]
        a_ref[pl.ds(hd, n_rows, stride=C_HEADS), :] = a
        b_ref[pl.ds(hd, n_rows, stride=C_HEADS), :] = _sqrt_nonneg(1.0 - a * a) * (hx * t_i + hx)


def _scan_rows(a_ref, b_ref, h_ref, h_init, n_rows, reverse, add_ref=None):
    n_blocks = n_rows // 4
    blk_rows = 4 * C_HEADS
    order = (3, 2, 1, 0) if reverse else (0, 1, 2, 3)

    def body(s, h):
        blk = n_blocks - 1 - s if reverse else s
        rows = pl.ds(pl.multiple_of(blk * blk_rows, blk_rows), blk_rows)
        a_blk = a_ref[rows, :]
        b_blk = b_ref[rows, :]
        a = [a_blk[i * C_HEADS:(i + 1) * C_HEADS] for i in order]
        b = [b_blk[i * C_HEADS:(i + 1) * C_HEADS] for i in order]
        a01, b01 = a[1] * a[0], a[1] * b[0] + b[1]
        a23, b23 = a[3] * a[2], a[3] * b[2] + b[3]
        a03, b03 = a23 * a01, a23 * b01 + b23
        hs = [a[0] * h + b[0], a01 * h + b01, None, a03 * h + b03]
        hs[2] = a[2] * hs[1] + b[2]
        h_blk = jnp.concatenate([hs[order.index(i)] for i in range(4)], axis=0)
        h_ref[rows, :] = h_blk if add_ref is None else h_blk + add_ref[rows, :]
        return hs[3]
    return lax.fori_loop(0, n_blocks, body, h_init, unroll=2)


def _head_rows(h_ref, hd, n_rows):
    return h_ref[pl.ds(hd, n_rows, stride=C_HEADS), :]


def _time_major(h_ref, n_rows):
    return jnp.concatenate([_head_rows(h_ref, hd, n_rows) for hd in range(C_HEADS)], axis=1)


def _state_spec():
    return pl.BlockSpec((None, C_HEADS, C_HEAD_DIM), lambda b, j: (b, 0, 0))


def _scan_scratch(tile, n):
    return [pltpu.VMEM((tile * C_HEADS, C_HEAD_DIM), F32) for _ in range(n)]


def _lru_specs(l):
    return [
        _const_spec((None, 2, C_HEADS, C_HEAD_DIM, 2 * C_HEAD_DIM), (l, 0, 0, 0, 0)),
        _const_spec((None, 4, D), (l, 0, 0)),
        _const_spec((None, 2, D), (l, 0, 0)),
    ]


def _scan_kernel(xl_ref, xc_ref, xr_ref, mod_ref, wxc_ref, bxc_ref, wcv_ref, bcv_ref,
                 wg_ref, bg_ref, lam_ref, h0f_ref, h0b_ref,
                 xcv_ref, hs_ref, endf_ref, endb_ref,
                 zx_ref, a_ref, b_ref, carry_ref, *, tile, n_tiles, dirs, reverse):
    j = pl.program_id(1)
    jj = n_tiles - 1 - j if reverse else j
    shift = mod_ref[3:4, :]
    scale = mod_ref[4:5, :]
    xe = jnp.concatenate([xl_ref[...], xc_ref[...], xr_ref[...]], axis=0)
    he = (xe * (1.0 + scale) + shift).astype(BF16)
    zx_ref[...] = _dot(he, wxc_ref[...]) + bxc_ref[...]
    zx_ref[0:HALO_C, :] = jnp.where(jj > 0, zx_ref[0:HALO_C, :], 0.0)
    zx_ref[tile + HALO_C:, :] = jnp.where(jj < n_tiles - 1, zx_ref[tile + HALO_C:, :], 0.0)
    acc = bcv_ref[...]
    for k in range(CONV_C):
        off = HALO_C - 2 + k
        base, s = off - off % SUBLANES, off % SUBLANES
        if s == 0:
            tap = zx_ref[pl.ds(base, tile), :]
        else:
            tap = _shift_rows(zx_ref[pl.ds(base, tile + SUBLANES), :], s, tile)
        acc = acc + wcv_ref[k:k + 1, :] * tap
    xcv_ref[...] = acc

    for d, end_ref in ((0, endf_ref), (1, endb_ref)):
        if d not in dirs:
            end_ref[...] = jnp.zeros_like(end_ref)
    first = True
    for d in dirs:
        _lru_coeffs(xcv_ref, wg_ref, bg_ref, lam_ref, a_ref, b_ref, d)
        h0_ref = h0b_ref if d == 1 else h0f_ref
        end_ref = endb_ref if d == 1 else endf_ref

        crows = slice(d * C_HEADS, (d + 1) * C_HEADS)

        @pl.when(j == 0)
        def _():
            carry_ref[crows, :] = h0_ref[...]

        h = _scan_rows(a_ref, b_ref, hs_ref, carry_ref[crows, :], tile, reverse=(d == 1),
                       add_ref=None if first else hs_ref)
        carry_ref[crows, :] = h
        end_ref[...] = h
        first = False


def _scan_call(x3, w, h0f, h0b, *, l, mod_base, tile, dirs, reverse):
    bsz, s, _ = x3.shape
    n_tiles = s // tile
    hb = tile // HALO_C
    n_hblk = s // HALO_C

    def ctr(b, j):
        return (b, n_tiles - 1 - j if reverse else j, 0)

    def left(b, j):
        jj = n_tiles - 1 - j if reverse else j
        return (b, jnp.maximum(jj * hb - 1, 0), 0)

    def right(b, j):
        jj = n_tiles - 1 - j if reverse else j
        return (b, jnp.minimum((jj + 1) * hb, n_hblk - 1), 0)

    vec = _state_spec()
    return pl.pallas_call(
        functools.partial(_scan_kernel, tile=tile, n_tiles=n_tiles, dirs=dirs, reverse=reverse),
        grid=(bsz, n_tiles),
        in_specs=[
            pl.BlockSpec((None, HALO_C, D), left),
            pl.BlockSpec((None, tile, D), ctr),
            pl.BlockSpec((None, HALO_C, D), right),
            pl.BlockSpec((None, None, N_MOD, D), lambda b, j: (l, mod_base(b), 0, 0)),
            _const_spec((None, D, D), (l, 0, 4)),
            _const_spec((None, 1, D), (l, 0, 4)),
            _const_spec((None, CONV_C, D), (l, 0, 0)),
            _layer_vec(l),
            *_lru_specs(l),
            vec, vec,
        ],
        out_specs=[
            pl.BlockSpec((None, tile, D), ctr),
            pl.BlockSpec((None, tile * C_HEADS, C_HEAD_DIM), ctr),
            vec, vec,
        ],
        out_shape=[
            jax.ShapeDtypeStruct((bsz, s, D), F32),
            jax.ShapeDtypeStruct((bsz, s * C_HEADS, C_HEAD_DIM), F32),
            jax.ShapeDtypeStruct((bsz, C_HEADS, C_HEAD_DIM), F32),
            jax.ShapeDtypeStruct((bsz, C_HEADS, C_HEAD_DIM), F32),
        ],
        scratch_shapes=[
            pltpu.VMEM((tile + 2 * HALO_C, D), F32),
            *_scan_scratch(tile, 2),
            pltpu.VMEM((2 * C_HEADS, C_HEAD_DIM), F32),
        ],
        compiler_params=_params(2),
        name="lru_scan",
    )(x3, x3, x3, w["mod"], w["win"], w["bin"], w["wcv"], w["bcv"],
      w["wg"], w["bg"], w["lam"], h0f, h0b)


def _conv31(hb_ref, wdw_ref, n_rows):
    groups = {}
    for k in range(CONV_W):
        off = k + HALO_B - CONV_HALF
        groups.setdefault(off % SUBLANES, []).append((k, off - off % SUBLANES))
    acc = None
    for s, taps in sorted(groups.items()):
        n = n_rows if s == 0 else n_rows + SUBLANES
        g = None
        for k, base in taps:
            term = wdw_ref[k:k + 1, :] * hb_ref[pl.ds(base, n), :]
            g = term if g is None else g + term
        part = g if s == 0 else _shift_rows(g, s, n_rows)
        acc = part if acc is None else acc + part
    return acc


def _branch_kernel(xl_ref, xc_ref, xr_ref, mod_ref, win_ref, bin_ref,
                   alg_ref, alb_ref, ws_ref, bs_ref, wdw_ref, bdw_ref, blg_ref, blb_ref, wbr_ref,
                   part_ref, gcg_ref, sgc_ref,
                   hb_ref, yb_ref, v_ref, vn_ref, ya_ref, *, tile, n_tiles):
    j = pl.program_id(1)
    shift = mod_ref[3:4, :]
    scale = mod_ref[4:5, :]
    xe = jnp.concatenate([xl_ref[...], xc_ref[...], xr_ref[...]], axis=0)
    he = (xe * (1.0 + scale) + shift).astype(BF16)
    hc = he[HALO_B:HALO_B + tile]

    def proj(lhs, group, cols):
        lo = group * D + cols.start
        return _dot(lhs, win_ref[:, lo:lo + CW]) + bin_ref[:, lo:lo + CW]

    chunks = [slice(c * CW, (c + 1) * CW) for c in range(D // CW)]

    for cols in chunks:
        hb_ref[:, cols] = proj(he, 2, cols) * _sigmoid(proj(he, 3, cols))
    hb_ref[0:HALO_B, :] = jnp.where(j > 0, hb_ref[0:HALO_B, :], 0.0)
    hb_ref[tile + HALO_B:, :] = jnp.where(j < n_tiles - 1, hb_ref[tile + HALO_B:, :], 0.0)
    acc = _conv31(hb_ref, wdw_ref, tile) + bdw_ref[...]
    yb_ref[...] = _silu(_ln(acc, blg_ref[...], blb_ref[...])).astype(BF16)

    for cols in chunks:
        v_ref[:, cols] = proj(hc, 1, cols)
    vn_ref[...] = _ln(v_ref[...], alg_ref[...], alb_ref[...]).astype(BF16)
    for g, cols in enumerate(chunks):
        u_a = proj(hc, 0, cols)
        for n in range(tile // CHUNK):
            rows = slice(n * CHUNK, (n + 1) * CHUNK)
            mixed = _dot(ws_ref[g], vn_ref[rows, cols]) + bs_ref[:, cols]
            ya_ref[rows, cols] = (u_a[rows, :] * mixed).astype(BF16)

    for cols in chunks:
        p_a = _dot(ya_ref[...], wbr_ref[0, :, cols])
        p_b = _dot(yb_ref[...], wbr_ref[1, :, cols])
        part_ref[:, cols] = _sigmoid(proj(hc, 6, cols)) * p_a + _sigmoid(proj(hc, 7, cols)) * p_b
        sgc_ref[:, cols] = _sigmoid(proj(hc, 8, cols))
        gcg_ref[:, cols] = _gelu(proj(hc, 5, cols))


def _branch_call(x3, w, *, l, mod_base, tile):
    bsz, s, _ = x3.shape
    n_tiles = s // tile
    hb = tile // HALO_B
    n_hblk = s // HALO_B
    ctr = pl.BlockSpec((None, tile, D), lambda b, j: (b, j, 0))
    out = jax.ShapeDtypeStruct((bsz, s, D), F32)
    return pl.pallas_call(
        functools.partial(_branch_kernel, tile=tile, n_tiles=n_tiles),
        grid=(bsz, n_tiles),
        in_specs=[
            pl.BlockSpec((None, HALO_B, D), lambda b, j: (b, jnp.maximum(j * hb - 1, 0), 0)),
            ctr,
            pl.BlockSpec((None, HALO_B, D), lambda b, j: (b, jnp.minimum((j + 1) * hb, n_hblk - 1), 0)),
            pl.BlockSpec((None, None, N_MOD, D), lambda b, j: (l, mod_base(b), 0, 0)),
            _const_spec((None, D, N_MOD * D), (l, 0, 0)),
            _const_spec((None, 1, N_MOD * D), (l, 0, 0)),
            _layer_vec(l),
            _layer_vec(l),
            _const_spec((None, A_GROUPS, CHUNK, CHUNK), (l, 0, 0, 0)),
            _const_spec((None, CHUNK, D), (l, 0, 0)),
            _const_spec((None, CONV_W, D), (l, 0, 0)),
            _layer_vec(l),
            _layer_vec(l),
            _layer_vec(l),
            _const_spec((None, 2, D, D), (l, 0, 0, 0)),
        ],
        out_specs=[ctr, ctr, ctr],
        out_shape=[out, out, out],
        scratch_shapes=[
            pltpu.VMEM((tile + 2 * HALO_B, D), F32),
            pltpu.VMEM((tile, D), BF16),
            pltpu.VMEM((tile, D), F32),
            pltpu.VMEM((tile, D), BF16),
            pltpu.VMEM((tile, D), BF16),
        ],
        compiler_params=_params(2),
        name="mixer_branches",
    )(x3, x3, x3, w["mod"], w["win"], w["bin"], w["alg"], w["alb"], w["ws"], w["bs"],
      w["wdw"], w["bdw"], w["blg"], w["blb"], w["wbr"])


def _merge_tail(x_ref, r, part_ref, gcg_ref, sgc_ref, mod_ref, wbc_ref, wo_ref, lng_ref, lnb_ref,
                o_ref):
    chunks = [slice(c * CW, (c + 1) * CW) for c in range(D // CW)]
    y_c = (r * gcg_ref[...]).astype(BF16)
    m = jnp.concatenate(
        [(part_ref[:, cols] + sgc_ref[:, cols] * _dot(y_c, wbc_ref[:, cols])).astype(BF16)
         for cols in chunks], axis=1)
    res = jnp.concatenate(
        [ALPHA * x_ref[:, cols] + mod_ref[5:6, cols] * _dot(m, wo_ref[:, cols]) for cols in chunks],
        axis=1)
    o_ref[...] = _ln(res, lng_ref[...], lnb_ref[...])


def _merge_scan_kernel(x_ref, xcv_ref, hs_ref, part_ref, gcg_ref, sgc_ref, mod_ref,
                       wg_ref, bg_ref, lam_ref, h0f_ref, wbc_ref, wo_ref, lng_ref, lnb_ref,
                       o_ref, a_ref, b_ref, hf_ref, carry_ref, *, tile):
    _lru_coeffs(xcv_ref, wg_ref, bg_ref, lam_ref, a_ref, b_ref, 0)

    @pl.when(pl.program_id(1) == 0)
    def _():
        carry_ref[...] = h0f_ref[...]

    carry_ref[...] = _scan_rows(a_ref, b_ref, hf_ref, carry_ref[...], tile, reverse=False,
                                add_ref=hs_ref)
    _merge_tail(x_ref, _time_major(hf_ref, tile), part_ref, gcg_ref, sgc_ref, mod_ref,
                wbc_ref, wo_ref, lng_ref, lnb_ref, o_ref)


def _merge_kernel(x_ref, hs_ref, part_ref, gcg_ref, sgc_ref, mod_ref,
                  wbc_ref, wo_ref, lng_ref, lnb_ref, o_ref, *, tile):
    _merge_tail(x_ref, _time_major(hs_ref, tile), part_ref, gcg_ref, sgc_ref, mod_ref,
                wbc_ref, wo_ref, lng_ref, lnb_ref, o_ref)


def _merge_call(x3, hs, part, gcg, sgc, w, *, l, mod_base, tile, xcv=None, h0f=None):
    bsz, s, _ = x3.shape
    ctr = pl.BlockSpec((None, tile, D), lambda b, j: (b, j, 0))
    mod_spec = pl.BlockSpec((None, None, N_MOD, D), lambda b, j: (l, mod_base(b), 0, 0))
    tail_specs = [
        _const_spec((None, None, D, D), (l, 2, 0, 0)),
        _const_spec((None, D, D), (l, 0, 0)),
        _const_spec((None, None, 1, D), (l, 1, 0, 0)),
        _const_spec((None, None, 1, D), (l, 1, 0, 0)),
    ]
    tail_args = (w["wbr"], w["wo"], w["lng"], w["lnb"])
    hs_spec = pl.BlockSpec((None, tile * C_HEADS, C_HEAD_DIM), lambda b, j: (b, j, 0))
    if xcv is None:
        body = functools.partial(_merge_kernel, tile=tile)
        in_specs = [ctr, hs_spec, ctr, ctr, ctr, mod_spec, *tail_specs]
        args = (x3, hs, part, gcg, sgc, w["mod"], *tail_args)
        scratch = []
    else:
        body = functools.partial(_merge_scan_kernel, tile=tile)
        in_specs = [ctr, ctr, hs_spec, ctr, ctr, ctr, mod_spec, *_lru_specs(l),
                    _state_spec(), *tail_specs]
        args = (x3, xcv, hs, part, gcg, sgc, w["mod"], w["wg"], w["bg"], w["lam"], h0f, *tail_args)
        scratch = [*_scan_scratch(tile, 3), pltpu.VMEM((C_HEADS, C_HEAD_DIM), F32)]
    return pl.pallas_call(
        body,
        grid=(bsz, s // tile),
        in_specs=in_specs,
        out_specs=ctr,
        out_shape=jax.ShapeDtypeStruct((bsz, s, D), F32),
        scratch_shapes=scratch,
        compiler_params=_params(2),
        name="mixer_merge",
    )(*args)


def kernel(x, c, ctx, c_ctx, w_ada, b_ada, ln_post_g, ln_post_b, ffn_w_up, ffn_w_down,
           w_in, b_in, a_ln_g, a_ln_b, a_w_s, a_b_s, b_w_dw, b_b_dw, b_ln_g, b_ln_b,
           c_w_conv, c_b_conv, c_w_a, c_b_a, c_w_x, c_b_x, c_lam, w_br, w_o):
    bsz, seq, _ = x.shape
    ctx_len = ctx.shape[1]

    cv = jnp.concatenate(
        [c_ctx[None, :], c, jnp.zeros((MOD_ROWS - 1 - bsz, D), F32)], axis=0)
    row = lambda v: v.reshape(DEPTH, 1, -1)
    w = {
        "mod": _modulation(cv, w_ada, b_ada).reshape(DEPTH, MOD_ROWS, N_MOD, D),
        "wup": ffn_w_up.astype(BF16),
        "wdn": ffn_w_down.astype(BF16),
        "lng": ln_post_g.reshape(DEPTH, 3, 1, D),
        "lnb": ln_post_b.reshape(DEPTH, 3, 1, D),
        "win": w_in.astype(BF16),
        "bin": row(b_in),
        "alg": row(a_ln_g), "alb": row(a_ln_b),
        "ws": a_w_s.astype(BF16),
        "bs": jnp.repeat(jnp.swapaxes(a_b_s, 1, 2), A_GROUP_W, axis=2),
        "wdw": b_w_dw, "bdw": row(b_b_dw), "blg": row(b_ln_g), "blb": row(b_ln_b),
        "wcv": c_w_conv, "bcv": row(c_b_conv),
        "wg": (0.5 * jnp.concatenate([c_w_a, c_w_x], axis=-1)).astype(BF16),
        "bg": jnp.stack([c_b_a[:, 0], c_b_x[:, 0], c_b_a[:, 1], c_b_x[:, 1]], axis=1),
        "lam": c_lam,
        "wbr": w_br.astype(BF16),
        "wo": w_o.astype(BF16),
    }
    lat_mod = lambda b: 1 + b
    ctx_mod = lambda b: 0
    zeros_h = jnp.zeros((bsz, C_HEADS, C_HEAD_DIM), F32)

    def ffn_lat(v, l, half):
        return _ffn(v.reshape(bsz * seq, D), w, l=l, half=half, rows_per_mod=seq,
                    mod_base=1).reshape(bsz, seq, D)

    def ffn_ctx(v, l, half):
        return _ffn(v.reshape(bsz * ctx_len, D), w, l=l, half=half, rows_per_mod=bsz * ctx_len,
                    mod_base=0).reshape(bsz, ctx_len, D)

    for l in range(DEPTH):
        ctx_out = l < DEPTH - 1
        x = ffn_lat(x, l, 0)
        ctx = ffn_ctx(ctx, l, 0)

        _, hs_ctx, end_f, end_b = _scan_call(
            ctx, w, zeros_h, zeros_h, l=l, mod_base=ctx_mod, tile=CTX_TILE, dirs=(0, 1),
            reverse=False)
        xcv_lat, hb_lat, _, _ = _scan_call(
            x, w, end_f, end_b, l=l, mod_base=lat_mod, tile=SCAN_TILE, dirs=(1,), reverse=True)

        part, gcg, sgc = _branch_call(x, w, l=l, mod_base=lat_mod, tile=LAT_TILE)
        x = _merge_call(x, hb_lat, part, gcg, sgc, w, l=l, mod_base=lat_mod, tile=LAT_TILE,
                        xcv=xcv_lat, h0f=end_f)
        x = ffn_lat(x, l, 1)

        if ctx_out:
            part, gcg, sgc = _branch_call(ctx, w, l=l, mod_base=ctx_mod, tile=CTX_TILE)
            ctx = _merge_call(ctx, hs_ctx, part, gcg, sgc, w, l=l, mod_base=ctx_mod,
                              tile=CTX_TILE)
            ctx = ffn_ctx(ctx, l, 1)
    return x
```
